```python
import jax, jax.numpy as jnp
from jax import lax
import numpy as np

D_MODEL = 2048
BATCH = 2
SEQ = 16384
DEPTH = 1

CTX_LEN = 256
GRID_W = 64
D_MIX = D_MODEL
GLA_HEADS = 4
GLA_WIDTH = D_MIX // 2
GLA_DV = GLA_WIDTH // GLA_HEADS
GLA_DK = GLA_DV // 2
GLA_QK = GLA_HEADS * GLA_DK
GATE_RANK = 16
GATE_NORMALIZER = 16.0
CHUNK = 64
FOURIER_WIDTH = D_MIX - GLA_WIDTH
FOURIER_GROUPS = 8
FOURIER_GROUP_DIM = FOURIER_WIDTH // FOURIER_GROUPS
N_EXPERTS = 16
CAPACITY_FACTOR = 2
D_EXPERT = D_MODEL // 2
EPS = 1e-6
PROJ_WIDTH = 2 * GLA_QK + 2 * GLA_WIDTH + 2 * GATE_RANK + FOURIER_WIDTH
SPLIT_POINTS = (GLA_QK, 2 * GLA_QK, 2 * GLA_QK + GLA_WIDTH, 2 * GLA_QK + 2 * GLA_WIDTH,
                2 * GLA_QK + 2 * GLA_WIDTH + GATE_RANK, 2 * GLA_QK + 2 * GLA_WIDTH + 2 * GATE_RANK)

kernel_name = 'hybrid_gla_fnet_ec_moe_diffusion_block'


def rms_norm(x, g):
    xf = x.astype(jnp.float32)
    y = xf * lax.rsqrt(jnp.mean(xf * xf, axis=-1, keepdims=True) + EPS)
    return (y * g.astype(jnp.float32)).astype(x.dtype)


def modulate(h, shift, scale):
    return h * (1 + scale) + shift


def ada_params(cond, w_ada, b_ada):
    return jnp.split(jax.nn.silu(cond) @ w_ada + b_ada, 6, axis=-1)


def split_projection(p, w_a2_f, b_a2_f, w_a2_b, b_a2_b):
    q, k, v, g, z_f, z_b, u = jnp.split(p, SPLIT_POINTS, axis=-1)
    log_a_f = jax.nn.log_sigmoid((z_f @ w_a2_f + b_a2_f).astype(jnp.float32)) / GATE_NORMALIZER
    log_a_b = jax.nn.log_sigmoid((z_b @ w_a2_b + b_a2_b).astype(jnp.float32)) / GATE_NORMALIZER
    heads = lambda t: t.reshape(t.shape[0], t.shape[1], GLA_HEADS, -1).transpose(0, 2, 1, 3)
    return heads(q), heads(k), heads(v), g, heads(log_a_f), heads(log_a_b), u


def gla_chunked(q, k, v, log_a, s0):
    b, h, l, dk = q.shape
    dv = v.shape[-1]
    n = l // CHUNK
    q = (q.astype(jnp.float32) * dk ** -0.5).reshape(b, h, n, CHUNK, dk)
    k = k.astype(jnp.float32).reshape(b, h, n, CHUNK, dk)
    v = v.astype(jnp.float32).reshape(b, h, n, CHUNK, dv)
    cum = jnp.cumsum(log_a.astype(jnp.float32).reshape(b, h, n, CHUNK, dk), axis=3)
    cum_last = cum[:, :, :, -1:, :]
    q_dec = q * jnp.exp(cum)
    k_inv = k * jnp.exp(-cum)
    k_to_end = k * jnp.exp(cum_last - cum)
    in_chunk = jnp.tril(jnp.ones((CHUNK, CHUNK), dtype=bool))
    scores = jnp.where(in_chunk, jnp.einsum('bhnid,bhnjd->bhnij', q_dec, k_inv), 0.0)
    o_intra = jnp.einsum('bhnij,bhnjv->bhniv', scores, v)
    chunk_decay = jnp.exp(cum_last[:, :, :, 0, :])
    xs = tuple(jnp.moveaxis(t, 2, 0) for t in (q_dec, k_to_end, v, chunk_decay))

    def step(state, inp):
        q_n, k_n, v_n, d_n = inp
        o_n = jnp.einsum('bhid,bhdv->bhiv', q_n, state)
        state = d_n[..., None] * state + jnp.einsum('bhjd,bhjv->bhdv', k_n, v_n)
        return state, o_n

    s_final, o_inter = lax.scan(step, s0.astype(jnp.float32), xs)
    o = o_intra + jnp.moveaxis(o_inter, 0, 2)
    return o.reshape(b, h, l, dv), s_final


def gla_bidirectional(q, k, v, log_a_f, log_a_b, s0_f, s0_b):
    rev = lambda t: jnp.flip(t, axis=2)
    o_f, s_f = gla_chunked(q, k, v, log_a_f, s0_f)
    o_b, s_b = gla_chunked(rev(q), rev(k), rev(v), rev(log_a_b), s0_b)
    return o_f + rev(o_b), s_f, s_b


def fourier_mix(u):
    b, l, _ = u.shape
    uf = u.astype(jnp.float32).reshape(b, l, FOURIER_GROUPS, FOURIER_GROUP_DIM)
    y = jnp.fft.fft2(uf, axes=(1, 3), norm='ortho').real
    return y.reshape(b, l, FOURIER_WIDTH).astype(u.dtype)


def merge_heads(o, g, u, gla_norm_g, w_out):
    b, h, l, dv = o.shape
    o = o.transpose(0, 2, 1, 3)
    o = o * lax.rsqrt(jnp.mean(o * o, axis=-1, keepdims=True) + EPS)
    o = o * gla_norm_g.reshape(GLA_HEADS, GLA_DV).astype(jnp.float32)
    y_gla = (o.reshape(b, l, GLA_WIDTH) * jax.nn.silu(g.astype(jnp.float32))).astype(g.dtype)
    return jnp.concatenate([y_gla, fourier_mix(u)], axis=-1) @ w_out


def ec_moe(h, w_router, w_gate, w_up, w_down):
    b, l, d = h.shape
    cap = CAPACITY_FACTOR * l // N_EXPERTS
    aff = jax.nn.softmax(jnp.einsum('bld,de->ble', h, w_router).astype(jnp.float32), axis=-1)
    gate, idx = lax.top_k(jnp.swapaxes(aff, 1, 2), cap)
    xe = jax.vmap(lambda hb, ib: hb[ib])(h, idx)
    hid = jax.nn.silu(jnp.einsum('becd,edf->becf', xe, w_gate)) * jnp.einsum('becd,edf->becf', xe, w_up)
    ye = jnp.einsum('becf,efd->becd', hid, w_down) * gate[..., None].astype(h.dtype)
    combine = lambda yb, ib: jnp.zeros((l, d), yb.dtype).at[ib.reshape(-1)].add(yb.reshape(-1, d))
    return jax.vmap(combine)(ye, idx)


def setup_inputs(seed: int = 0) -> dict:
    key = jax.random.key(seed)
    ks = jax.random.split(key, 24)
    nrm = lambda k, shape, s: jax.random.normal(k, shape, jnp.float32) * s
    D, L = DEPTH, D_MODEL
    return {
        'x': nrm(ks[0], (BATCH, SEQ, D_MODEL), 1.0),
        'c': nrm(ks[1], (BATCH, D_MODEL), 1.0),
        'ctx': nrm(ks[2], (BATCH, CTX_LEN, D_MODEL), 1.0),
        'c_ctx': nrm(ks[3], (D_MODEL,), 1.0),
        'w_ada': nrm(ks[4], (D, D_MODEL, 6 * D_MODEL), D_MODEL ** -0.5),
        'b_ada': nrm(ks[5], (D, 6 * D_MODEL), 0.02),
        'norm1_g': 1.0 + nrm(ks[6], (D, D_MODEL), 0.02),
        'w_in': nrm(ks[7], (D, D_MODEL, PROJ_WIDTH), D_MODEL ** -0.5),
        'w_a2_f': nrm(ks[8], (D, GATE_RANK, GLA_QK), GATE_RANK ** -0.5),
        'b_a2_f': nrm(ks[9], (D, GLA_QK), 0.1),
        'w_a2_b': nrm(ks[10], (D, GATE_RANK, GLA_QK), GATE_RANK ** -0.5),
        'b_a2_b': nrm(ks[11], (D, GLA_QK), 0.1),
        'gla_norm_g': 1.0 + nrm(ks[12], (D, GLA_WIDTH), 0.02),
        'w_out': nrm(ks[13], (D, D_MIX, D_MODEL), D_MIX ** -0.5),
        'norm2_g': 1.0 + nrm(ks[14], (D, D_MODEL), 0.02),
        'w_router': nrm(ks[15], (D, D_MODEL, N_EXPERTS), D_MODEL ** -0.5),
        'w_e_gate': nrm(ks[16], (D, N_EXPERTS, D_MODEL, D_EXPERT), D_MODEL ** -0.5),
        'w_e_up': nrm(ks[17], (D, N_EXPERTS, D_MODEL, D_EXPERT), D_MODEL ** -0.5),
        'w_e_down': nrm(ks[18], (D, N_EXPERTS, D_EXPERT, D_MODEL), D_EXPERT ** -0.5),
        'final_norm_g': 1.0 + nrm(ks[19], (D_MODEL,), 0.02),
    }


def reference(x, c, ctx, c_ctx, w_ada, b_ada, norm1_g, w_in, w_a2_f, b_a2_f, w_a2_b, b_a2_b,
              gla_norm_g, w_out, norm2_g, w_router, w_e_gate, w_e_up, w_e_down, final_norm_g):
    b, l, _ = x.shape
    rows = l // GRID_W
    assert rows * GRID_W == l
    zero_state = jnp.zeros((b, GLA_HEADS, GLA_DK, GLA_DV), jnp.float32)
    x_lat, x_ctx = x, ctx
    for layer in range(DEPTH):
        is_last = layer == DEPTH - 1
        sh1_l, sc1_l, g1_l, sh2_l, sc2_l, g2_l = [m[:, None, :] for m in ada_params(c, w_ada[layer], b_ada[layer])]
        sh1_c, sc1_c, g1_c, sh2_c, sc2_c, g2_c = ada_params(c_ctx, w_ada[layer], b_ada[layer])
        gate_w = (w_a2_f[layer], b_a2_f[layer], w_a2_b[layer], b_a2_b[layer])

        h_c = modulate(rms_norm(x_ctx, norm1_g[layer]), sh1_c, sc1_c)
        q_c, k_c, v_c, gt_c, laf_c, lab_c, u_c = split_projection(h_c @ w_in[layer], *gate_w)
        o_c, s_f, s_b = gla_bidirectional(q_c, k_c, v_c, laf_c, lab_c, zero_state, zero_state)

        h_l = modulate(rms_norm(x_lat, norm1_g[layer]), sh1_l, sc1_l)
        q_l, k_l, v_l, gt_l, laf_l, lab_l, u_l = split_projection(h_l @ w_in[layer], *gate_w)
        o_l, _, _ = gla_bidirectional(q_l, k_l, v_l, laf_l, lab_l, s_f, s_b)
        x_lat = x_lat + g1_l * merge_heads(o_l, gt_l, u_l, gla_norm_g[layer], w_out[layer])

        if not is_last:
            x_ctx = x_ctx + g1_c * merge_heads(o_c, gt_c, u_c, gla_norm_g[layer], w_out[layer])
            hf_c = modulate(rms_norm(x_ctx, norm2_g[layer]), sh2_c, sc2_c)
            x_ctx = x_ctx + g2_c * ec_moe(hf_c, w_router[layer], w_e_gate[layer], w_e_up[layer], w_e_down[layer])

        hf_l = modulate(rms_norm(x_lat, norm2_g[layer]), sh2_l, sc2_l)
        x_lat = x_lat + g2_l * ec_moe(hf_l, w_router[layer], w_e_gate[layer], w_e_up[layer], w_e_down[layer])
    return rms_norm(x_lat, final_norm_g)
```

```python
import functools

import jax
import jax.numpy as jnp
import numpy as np
from jax import lax
from jax.experimental import pallas as pl
from jax.experimental.pallas import tpu as pltpu

F32 = jnp.float32
BF16 = jnp.bfloat16
I32 = jnp.int32
HIGHEST = lax.Precision.HIGHEST

EPS = 1e-6
GLA_HEADS = 4
GLA_DK = 128
GLA_DV = 256
GATE_RANK = 16
GATE_NORMALIZER = 16.0
FOURIER_GROUP_DIM = 128
N_EXPERTS = 16
CAPACITY_FACTOR = 2
LANES = 128
VMEM_LIMIT = 56 * 1024 * 1024


def _params(*sem):
    return pltpu.CompilerParams(dimension_semantics=sem, vmem_limit_bytes=VMEM_LIMIT)


def _resident(block_shape, index_map):
    return pl.BlockSpec(block_shape, index_map, pipeline_mode=pl.Buffered(1))


def _ada_kernel(c_ref, w_ref, b_ref, o_ref):
    c = c_ref[...]
    s = c * jax.nn.sigmoid(c)
    o_ref[...] = jnp.dot(s.astype(BF16), w_ref[...].astype(BF16),
                         preferred_element_type=F32) + b_ref[...]


def _ada(cond, w_ada, b_ada):
    rows, d = cond.shape
    n = w_ada.shape[1]
    tn = 1024
    return pl.pallas_call(
        _ada_kernel,
        out_shape=jax.ShapeDtypeStruct((rows, n), F32),
        grid=(n // tn,),
        in_specs=[pl.BlockSpec((rows, d), lambda j: (0, 0)),
                  pl.BlockSpec((d, tn), lambda j: (0, j)),
                  pl.BlockSpec((1, tn), lambda j: (0, j))],
        out_specs=pl.BlockSpec((rows, tn), lambda j: (0, j)),
        compiler_params=_params("arbitrary"),
        name="ada",
    )(cond, w_ada, b_ada.reshape(1, n))


def _rms_mod(x, g, shift, scale):
    y = x * lax.rsqrt(jnp.mean(x * x, axis=-1, keepdims=True) + EPS)
    return (y * g) * (1.0 + scale) + shift


def _inproj_kernel(x_ref, g_ref, sh_ref, sc_ref, wm_ref, wu_ref, wz_ref, om_ref, ou_ref, oz_ref):
    h = _rms_mod(x_ref[0], g_ref[...], sh_ref[0], sc_ref[0]).astype(BF16)
    n = wm_ref.shape[1]
    step = 1024
    for j in range(n // step):
        om_ref[0, :, j * step:(j + 1) * step] = jnp.dot(
            h, wm_ref[:, j * step:(j + 1) * step], preferred_element_type=F32).astype(BF16)
    ou_ref[0] = jnp.dot(h, wu_ref[...], preferred_element_type=F32)
    oz_ref[0] = jnp.dot(h, wz_ref[...], preferred_element_type=F32)


def _inproj(x, norm_g, shift, scale, w_main, w_u, w_z, tm):
    b, n, d = x.shape
    nm = w_main.shape[1]
    nu = w_u.shape[1]
    return pl.pallas_call(
        _inproj_kernel,
        out_shape=(jax.ShapeDtypeStruct((b, n, nm), BF16),
                   jax.ShapeDtypeStruct((b, n, nu), F32),
                   jax.ShapeDtypeStruct((b, n, LANES), F32)),
        grid=(b, n // tm),
        in_specs=[pl.BlockSpec((1, tm, d), lambda i, j: (i, j, 0)),
                  _resident((1, d), lambda i, j: (0, 0)),
                  pl.BlockSpec((1, 1, d), lambda i, j: (i, 0, 0)),
                  pl.BlockSpec((1, 1, d), lambda i, j: (i, 0, 0)),
                  _resident((d, nm), lambda i, j: (0, 0)),
                  _resident((d, nu), lambda i, j: (0, 0)),
                  _resident((d, LANES), lambda i, j: (0, 0))],
        out_specs=(pl.BlockSpec((1, tm, nm), lambda i, j: (i, j, 0)),
                   pl.BlockSpec((1, tm, nu), lambda i, j: (i, j, 0)),
                   pl.BlockSpec((1, tm, LANES), lambda i, j: (i, j, 0))),
        compiler_params=_params("parallel", "arbitrary"),
        name="inproj",
    )(x, norm_g.reshape(1, d), shift, scale, w_main, w_u, w_z)


def _gla_kernel(q_ref, k_ref, v_ref, z_ref, wa_ref, ba_ref, s0_ref, o_ref, sf_ref, st_ref, *,
                reverse, zoff):
    i = pl.program_id(1)

    @pl.when(i == 0)
    def _():
        st_ref[...] = s0_ref[0]

    r = q_ref.shape[1]
    z = z_ref[0][:, zoff:zoff + GATE_RANK]
    pre = jnp.dot(z, wa_ref[...], preferred_element_type=F32, precision=HIGHEST) + ba_ref[...]
    la = jax.nn.log_sigmoid(pre) * (1.0 / GATE_NORMALIZER)
    row = lax.broadcasted_iota(I32, (r, r), 0)
    col = lax.broadcasted_iota(I32, (r, r), 1)
    keep = (col >= row) if reverse else (col <= row)
    c = jnp.dot(keep.astype(F32), la, preferred_element_type=F32, precision=HIGHEST)
    tot = c[0:1, :] if reverse else c[r - 1:r, :]
    ref = 0.5 * tot
    e_in = jnp.exp(c)
    e_q = jnp.exp(c - ref)
    e_k = jnp.exp(ref - c)
    e_end = jnp.exp(tot - c)
    e_tot = jnp.exp(tot)
    q = q_ref[0].astype(F32) * (GLA_DK ** -0.5)
    k = k_ref[0].astype(F32)
    q_in = (q * e_in).astype(BF16)
    q_d = (q * e_q).astype(BF16)
    k_i = (k * e_k).astype(BF16)
    k_e = (k * e_end).astype(BF16)
    v = v_ref[0]
    eye = lax.broadcasted_iota(I32, (GLA_DK, GLA_DK), 0) == lax.broadcasted_iota(I32, (GLA_DK, GLA_DK), 1)
    for h in range(GLA_HEADS):
        ks = slice(h * GLA_DK, (h + 1) * GLA_DK)
        vs = slice(h * GLA_DV, (h + 1) * GLA_DV)
        s_in = st_ref[h]
        sc = lax.dot_general(q_d[:, ks], k_i[:, ks], (((1,), (1,)), ((), ())),
                             preferred_element_type=F32)
        sc = jnp.where(keep, sc, 0.0).astype(BF16)
        o = jnp.dot(sc, v[:, vs], preferred_element_type=F32)
        o = o + jnp.dot(q_in[:, ks], s_in.astype(BF16), preferred_element_type=F32)
        o_ref[0, :, vs] = o
        kv = lax.dot_general(k_e[:, ks], v[:, vs], (((0,), (0,)), ((), ())),
                             preferred_element_type=F32)
        dcol = jnp.sum(jnp.where(eye, e_tot[:, ks], 0.0), axis=1, keepdims=True)
        st_ref[h] = dcol * s_in + kv

    @pl.when(i == pl.num_programs(1) - 1)
    def _():
        sf_ref[0] = st_ref[...]


def _gla(main, z, wa, ba, s0, *, reverse, r):
    b, n, _ = main.shape
    nb = n // r
    hk = GLA_HEADS * GLA_DK
    hv = GLA_HEADS * GLA_DV
    pos = (lambda j: nb - 1 - j) if reverse else (lambda j: j)
    kern = functools.partial(_gla_kernel, reverse=reverse, zoff=GATE_RANK if reverse else 0)
    return pl.pallas_call(
        kern,
        out_shape=(jax.ShapeDtypeStruct((b, n, hv), F32),
                   jax.ShapeDtypeStruct((b, GLA_HEADS, GLA_DK, GLA_DV), F32)),
        grid=(b, nb),
        in_specs=[pl.BlockSpec((1, r, hk), lambda i, j: (i, pos(j), 0)),
                  pl.BlockSpec((1, r, hk), lambda i, j: (i, pos(j), 1)),
                  pl.BlockSpec((1, r, hv), lambda i, j: (i, pos(j), 1)),
                  pl.BlockSpec((1, r, LANES), lambda i, j: (i, pos(j), 0)),
                  _resident((GATE_RANK, hk), lambda i, j: (0, 0)),
                  _resident((1, hk), lambda i, j: (0, 0)),
                  pl.BlockSpec((1, GLA_HEADS, GLA_DK, GLA_DV), lambda i, j: (i, 0, 0, 0))],
        out_specs=(pl.BlockSpec((1, r, hv), lambda i, j: (i, pos(j), 0)),
                   pl.BlockSpec((1, GLA_HEADS, GLA_DK, GLA_DV), lambda i, j: (i, 0, 0, 0))),
        scratch_shapes=[pltpu.VMEM((GLA_HEADS, GLA_DK, GLA_DV), F32)],
        compiler_params=_params("parallel", "arbitrary"),
        name="gla_bwd" if reverse else "gla_fwd",
    )(main, main, main, z, wa, ba.reshape(1, hk), s0)


SUB = 8


def _four_a_kernel(x_ref, w_ref, o_ref):
    w = w_ref[...]
    for j in range(SUB):
        xj = x_ref[0, :, j, :].astype(BF16)
        a = jnp.dot(w, xj, preferred_element_type=F32).astype(BF16)
        o_ref[0, j] = a.reshape(2, a.shape[0] // 2, a.shape[-1])


def _four_a(u4, w1):
    b, n1, n2, ch = u4.shape
    return pl.pallas_call(
        _four_a_kernel,
        out_shape=jax.ShapeDtypeStruct((b, n2, 2, n1, ch), BF16),
        grid=(b, n2 // SUB),
        in_specs=[pl.BlockSpec((1, n1, SUB, ch), lambda i, j: (i, 0, j, 0)),
                  _resident((2 * n1, n1), lambda i, j: (0, 0))],
        out_specs=pl.BlockSpec((1, SUB, 2, n1, ch), lambda i, j: (i, j, 0, 0, 0)),
        compiler_params=_params("parallel", "arbitrary"),
        name="four_a",
    )(u4, w1)


def _four_b_kernel(re_ref, im_ref, w_ref, cs_ref, o_ref):
    ch = o_ref.shape[-1]
    ng = ch // FOURIER_GROUP_DIM
    cs = cs_ref[...]
    for j in range(SUB):
        a = jnp.concatenate([re_ref[0, :, j * ch:(j + 1) * ch], im_ref[0, :, j * ch:(j + 1) * ch]], axis=0)
        a2 = jnp.dot(w_ref[j], a, preferred_element_type=F32).astype(BF16)
        n2 = a2.shape[0] // 2
        lhs = jnp.concatenate(
            [jnp.concatenate([a2[:n2, g * LANES:(g + 1) * LANES], a2[n2:, g * LANES:(g + 1) * LANES]], axis=1)
             for g in range(ng)], axis=0)
        y = jnp.dot(lhs, cs, preferred_element_type=F32)
        y = jnp.concatenate([y[g * n2:(g + 1) * n2] for g in range(ng)], axis=1)
        o_ref[0, :, j, :] = y


def _four_b(a, w2, cs):
    b, n2, _, n1, ch = a.shape
    a2d = a.reshape(b, n2, 2 * n1 * ch)
    return pl.pallas_call(
        _four_b_kernel,
        out_shape=jax.ShapeDtypeStruct((b, n2, n1, ch), F32),
        grid=(b, n1 // SUB),
        in_specs=[pl.BlockSpec((1, n2, SUB * ch), lambda i, j: (i, 0, j)),
                  pl.BlockSpec((1, n2, SUB * ch), lambda i, j: (i, 0, n1 // SUB + j)),
                  pl.BlockSpec((SUB, 2 * n2, 2 * n2), lambda i, j: (j, 0, 0)),
                  _resident((2 * FOURIER_GROUP_DIM, FOURIER_GROUP_DIM), lambda i, j: (0, 0))],
        out_specs=pl.BlockSpec((1, n2, SUB, ch), lambda i, j: (i, 0, j, 0)),
        compiler_params=_params("parallel", "arbitrary"),
        name="four_b",
    )(a2d, a2d, w2, cs)


def _dft_tables(n):
    n1 = n // LANES

    def cs(num, den, count):
        ang = (num % den).astype(F32) * F32(2.0 * np.pi / den)
        scale = float(count) ** -0.5
        return jnp.cos(ang) * scale, jnp.sin(ang) * scale

    i1 = jnp.arange(n1, dtype=I32)
    il = jnp.arange(LANES, dtype=I32)
    c1, s1 = cs(i1[:, None] * i1[None, :], n1, n1)
    w1 = jnp.concatenate([c1, -s1], axis=0).astype(BF16)
    kk = i1[:, None, None] + n1 * il[None, :, None]
    c2, s2 = cs(kk * il[None, None, :], n, LANES)
    w2 = jnp.concatenate([jnp.concatenate([c2, s2], axis=2),
                          jnp.concatenate([-s2, c2], axis=2)], axis=1).astype(BF16)
    cc, sc = cs(il[:, None] * il[None, :], FOURIER_GROUP_DIM, FOURIER_GROUP_DIM)
    csm = jnp.concatenate([cc, sc], axis=0).astype(BF16)
    return w1, w2, csm


def _mix_kernel(of_ref, ob_ref, g_ref, yf_ref, x_ref, gn_ref, wo_ref, g1_ref, n2_ref, sh_ref, sc_ref,
                wr_ref, x1_ref, hf_ref, aff_ref):
    o = of_ref[0] + ob_ref[0]
    heads = []
    for h in range(GLA_HEADS):
        oh = o[:, h * GLA_DV:(h + 1) * GLA_DV]
        heads.append(oh * lax.rsqrt(jnp.mean(oh * oh, axis=-1, keepdims=True) + EPS))
    on = jnp.concatenate(heads, axis=1) * gn_ref[...]
    g = g_ref[0].astype(F32)
    y_gla = (on * (g * jax.nn.sigmoid(g))).astype(BF16)
    hw = y_gla.shape[1]
    acc = jnp.dot(y_gla, wo_ref[:hw, :], preferred_element_type=F32)
    acc = acc + jnp.dot(yf_ref[0].astype(BF16), wo_ref[hw:, :], preferred_element_type=F32)
    x1 = x_ref[0] + g1_ref[0] * acc
    x1_ref[0] = x1
    hf = _rms_mod(x1, n2_ref[...], sh_ref[0], sc_ref[0])
    hf_ref[0] = hf
    logits = lax.dot_general(wr_ref[...], hf, (((1,), (1,)), ((), ())),
                             preferred_element_type=F32, precision=HIGHEST)
    p = jnp.exp(logits - jnp.max(logits, axis=0, keepdims=True))
    aff_ref[0] = p / jnp.sum(p, axis=0, keepdims=True)


def _mix(o_f, o_b, main, yf, x, gla_norm_g, w_out, g1, norm2_g, sh2, sc2, w_router_t, tm):
    b, n, d = x.shape
    hv = o_f.shape[-1]
    ne = w_router_t.shape[0]
    row = lambda w: pl.BlockSpec((1, tm, w), lambda i, j: (i, j, 0))
    vec = pl.BlockSpec((1, 1, d), lambda i, j: (i, 0, 0))
    return pl.pallas_call(
        _mix_kernel,
        out_shape=(jax.ShapeDtypeStruct((b, n, d), F32),
                   jax.ShapeDtypeStruct((b, n, d), F32),
                   jax.ShapeDtypeStruct((b, ne, n), F32)),
        grid=(b, n // tm),
        in_specs=[row(hv), row(hv),
                  pl.BlockSpec((1, tm, hv), lambda i, j: (i, j, 2)),
                  row(yf.shape[-1]), row(d),
                  _resident((1, hv), lambda i, j: (0, 0)),
                  _resident(w_out.shape, lambda i, j: (0, 0)),
                  vec,
                  _resident((1, d), lambda i, j: (0, 0)),
                  vec, vec,
                  _resident((ne, d), lambda i, j: (0, 0))],
        out_specs=(row(d), row(d), pl.BlockSpec((1, ne, tm), lambda i, j: (i, 0, j))),
        compiler_params=_params("parallel", "arbitrary"),
        name="mix",
    )(o_f, o_b, main, yf, x, gla_norm_g.reshape(1, hv), w_out, g1, norm2_g.reshape(1, d), sh2, sc2,
      w_router_t)


def _per_expert_total(v, nk):
    ne = v.shape[1] // nk
    rows = jnp.concatenate([v[:, e * nk:(e + 1) * nk] for e in range(ne)], axis=0)
    tot = jnp.broadcast_to(jnp.sum(rows, axis=1, keepdims=True), rows.shape)
    return jnp.concatenate([tot[e:e + 1, :] for e in range(ne)], axis=1)


def _token_cumsum(m, nk):
    ne = m.shape[1] // nk
    tril = (lax.broadcasted_iota(I32, (LANES, LANES), 1) <= lax.broadcasted_iota(I32, (LANES, LANES), 0))
    within = jnp.dot(tril.astype(BF16), m.astype(BF16), preferred_element_type=F32)
    coltot = within[LANES - 1:LANES, :]
    rows = jnp.concatenate([coltot[:, e * nk:(e + 1) * nk] for e in range(ne)], axis=0)
    upper = (lax.broadcasted_iota(I32, (nk, nk), 0) < lax.broadcasted_iota(I32, (nk, nk), 1))
    offs = jnp.dot(rows.astype(BF16), upper.astype(BF16), preferred_element_type=F32)
    offs = jnp.concatenate([offs[e:e + 1, :] for e in range(ne)], axis=1)
    return within + offs


def _route_a_kernel(a_ref, cum_ref, pos_ref, start_ref, cnt_ref, *, nk, cap):
    a = a_ref[0]
    ne = a.shape[1] // nk

    def key_value(key):
        return pltpu.bitcast(jnp.where(key < 0, key ^ jnp.int32(0x7FFFFFFF), key), F32)

    def count_ge(key):
        colsum = jnp.sum(jnp.where(a >= key_value(key), 1.0, 0.0), axis=0, keepdims=True)
        return _per_expert_total(colsum, nk)

    fcap = jnp.float32(cap)
    int_min = jnp.full((1, a.shape[1]), -2 ** 31, I32)
    thr = jnp.where(count_ge(jnp.zeros_like(int_min)) >= fcap, 0, int_min)

    def step(i, thr):
        cand = thr + jnp.left_shift(jnp.int32(1), 30 - i)
        return jnp.where(count_ge(cand) >= fcap, cand, thr)

    thr = lax.fori_loop(0, 31, step, thr)
    gt = a >= key_value(thr + 1)
    eq = (a >= key_value(thr)) & jnp.logical_not(gt)
    n_gt = _per_expert_total(jnp.sum(jnp.where(gt, 1.0, 0.0), axis=0, keepdims=True), nk)
    eqf = jnp.where(eq, 1.0, 0.0)
    tie_rank = _token_cumsum(eqf, nk) - eqf
    sel = jnp.where(gt | (eq & (tie_rank < fcap - n_gt)), 1.0, 0.0)
    cum = _token_cumsum(sel, nk)
    cum_ref[0] = cum
    excl = cum - sel
    start = excl[:, 0:nk]
    cnt = sel[:, 0:nk]
    for e in range(1, ne):
        start = start + excl[:, e * nk:(e + 1) * nk]
        cnt = cnt + sel[:, e * nk:(e + 1) * nk]
    start_ref[0] = start
    cnt_ref[0] = cnt
    before = jnp.zeros_like(start)
    for e in range(ne):
        pos_ref[0, :, e * nk:(e + 1) * nk] = start + before
        before = before + sel[:, e * nk:(e + 1) * nk]


def _route_a(a_tab, nk, cap):
    b, _, w = a_tab.shape
    tab = lambda width: pl.BlockSpec((1, LANES, width), lambda i: (i, 0, 0))
    return pl.pallas_call(
        functools.partial(_route_a_kernel, nk=nk, cap=cap),
        out_shape=(jax.ShapeDtypeStruct((b, LANES, w), F32), jax.ShapeDtypeStruct((b, LANES, w), F32),
                   jax.ShapeDtypeStruct((b, LANES, nk), F32), jax.ShapeDtypeStruct((b, LANES, nk), F32)),
        grid=(b,),
        in_specs=[tab(w)],
        out_specs=(tab(w), tab(w), tab(nk), tab(nk)),
        compiler_params=_params("parallel"),
        name="route_a",
    )(a_tab)


def _route_b_kernel(cum_ref, a_ref, pos_ref, tok_ref, gate_ref, dst_ref, *, cap):
    cum = cum_ref[0]
    nk = cum.shape[1]
    col_end = cum[LANES - 1:LANES, :]
    eye = lax.broadcasted_iota(I32, (nk, nk), 0) == lax.broadcasted_iota(I32, (nk, nk), 1)
    col_end_c = jnp.sum(jnp.where(eye, col_end, 0.0), axis=1, keepdims=True)
    s = lax.broadcasted_iota(I32, (1, cap), 1).astype(F32)
    blk = jnp.sum(jnp.where(col_end_c <= s, 1.0, 0.0), axis=0, keepdims=True)
    pick_col = jnp.where(lax.broadcasted_iota(I32, (nk, cap), 0).astype(F32) == blk, 1.0, 0.0)
    gather = lambda tab: jnp.dot(tab, pick_col, preferred_element_type=F32, precision=HIGHEST)
    cum_rows = jnp.round(gather(cum))
    lpos = jnp.sum(jnp.where(cum_rows <= s, 1.0, 0.0), axis=0, keepdims=True)
    pick_row = lax.broadcasted_iota(I32, (LANES, cap), 0).astype(F32) == lpos
    tok_ref[0, 0] = (blk * LANES + lpos).astype(I32)
    gate_ref[0, 0] = jnp.sum(jnp.where(pick_row, gather(a_ref[0]), 0.0), axis=0, keepdims=True)
    dst = jnp.sum(jnp.where(pick_row, gather(pos_ref[0]), 0.0), axis=0, keepdims=True)
    dst_ref[0, 0] = jnp.round(dst).astype(I32)


def _route_b(cum, a_tab, pos, nk, cap):
    b, _, w = cum.shape
    ne = w // nk
    tab = pl.BlockSpec((1, LANES, nk), lambda i, e: (i, 0, e))
    slot = pl.BlockSpec((1, 1, 1, cap), lambda i, e: (i, e, 0, 0))
    sds = lambda dt: jax.ShapeDtypeStruct((b, ne, 1, cap), dt)
    return pl.pallas_call(
        functools.partial(_route_b_kernel, cap=cap),
        out_shape=(sds(I32), sds(F32), sds(I32)),
        grid=(b, ne),
        in_specs=[tab, tab, tab],
        out_specs=(slot, slot, slot),
        compiler_params=_params("parallel", "arbitrary"),
        name="route_b",
    )(cum, a_tab, pos)


def _expert_kernel(tok_ref, dst_ref, hf_ref, gate_ref, wg_ref, wu_ref, wd_ref, z_ref,
                   xbuf, ybuf, gsem, ssem):
    b = pl.program_id(1)
    m = xbuf.shape[0]
    base = pl.program_id(2) * m

    def gather_row(i, carry):
        t = tok_ref[0, 0, base + i]
        pltpu.make_async_copy(hf_ref.at[b, pl.ds(t, 1), :], xbuf.at[pl.ds(i, 1), :], gsem).start()
        return carry

    lax.fori_loop(0, m, gather_row, 0, unroll=8)
    pltpu.make_async_copy(hf_ref.at[b, pl.ds(0, m), :], xbuf, gsem).wait()
    x = xbuf[...].astype(BF16)
    h1 = jnp.dot(x, wg_ref[0], preferred_element_type=F32)
    h2 = jnp.dot(x, wu_ref[0], preferred_element_type=F32)
    hid = (h1 * jax.nn.sigmoid(h1) * h2).astype(BF16)
    ybuf[...] = jnp.dot(hid, wd_ref[0], preferred_element_type=F32) * gate_ref[0, 0]

    def scatter_row(i, carry):
        p = dst_ref[0, 0, base + i]
        pltpu.make_async_copy(ybuf.at[pl.ds(i, 1), :], z_ref.at[b, pl.ds(p, 1), :], ssem).start()
        return carry

    lax.fori_loop(0, m, scatter_row, 0, unroll=8)
    pltpu.make_async_copy(ybuf, z_ref.at[b, pl.ds(0, m), :], ssem).wait()


def _expert(tok, dst, hf, gate_col, w_gate, w_up, w_down, m):
    b, n, d = hf.shape
    ne, _, f = w_gate.shape
    cap = tok.shape[-1]
    idx = pl.BlockSpec((1, 1, cap), lambda e, i, j: (i * ne + e, 0, 0), memory_space=pltpu.SMEM)
    return pl.pallas_call(
        _expert_kernel,
        out_shape=jax.ShapeDtypeStruct((b, ne * cap, d), F32),
        grid=(ne, b, cap // m),
        in_specs=[idx, idx,
                  pl.BlockSpec(memory_space=pl.ANY),
                  pl.BlockSpec((1, 1, m, 1), lambda e, i, j: (i, e, j, 0)),
                  pl.BlockSpec((1, d, f), lambda e, i, j: (e, 0, 0)),
                  pl.BlockSpec((1, d, f), lambda e, i, j: (e, 0, 0)),
                  pl.BlockSpec((1, f, d), lambda e, i, j: (e, 0, 0))],
        out_specs=pl.BlockSpec(memory_space=pl.ANY),
        scratch_shapes=[pltpu.VMEM((m, d), F32), pltpu.VMEM((m, d), F32),
                        pltpu.SemaphoreType.DMA, pltpu.SemaphoreType.DMA],
        compiler_params=_params("arbitrary", "arbitrary", "arbitrary"),
        name="expert",
    )(tok.reshape(b * ne, 1, cap), dst.reshape(b * ne, 1, cap), hf, gate_col, w_gate, w_up, w_down)


def _combine_kernel(r0_ref, r1_ref, z_ref, st_ref, cn_ref, x1_ref, g2_ref, fg_ref, o_ref, zbuf, acc, sem):
    b = pl.program_id(0)
    i = pl.program_id(1)
    nt = pl.num_programs(1)
    kc = zbuf.shape[0]
    zrows = z_ref.shape[1]
    r0 = r0_ref[b * nt + i]
    r1 = r1_ref[b * nt + i]
    r0a = (r0 // SUB) * SUB
    start = st_ref[0]
    end = start + cn_ref[0]
    acc[...] = jnp.zeros_like(acc)

    def chunk(c, carry):
        nominal = r0a + c * kc
        rb = pl.multiple_of(jnp.minimum(nominal, zrows - kc), SUB)
        cp = pltpu.make_async_copy(z_ref.at[b, pl.ds(rb, kc), :], zbuf, sem)
        cp.start()
        rid = lax.broadcasted_iota(I32, (1, kc), 1) + rb
        ridf = rid.astype(F32)
        mine = (ridf >= start) & (ridf < end) & (rid >= nominal)
        p = jnp.where(mine, 1.0, 0.0).astype(BF16)
        cp.wait()
        acc[...] += jnp.dot(p, zbuf[...].astype(BF16), preferred_element_type=F32)
        return carry

    lax.fori_loop(0, (r1 - r0a + kc - 1) // kc, chunk, 0)
    y = x1_ref[0] + g2_ref[0] * acc[...]
    o_ref[0] = y * lax.rsqrt(jnp.mean(y * y, axis=-1, keepdims=True) + EPS) * fg_ref[...]


def _combine(r0, r1, z, start_col, cnt_col, x1, g2, final_g, t, kc):
    b, n, d = x1.shape
    nt = n // t
    col = pl.BlockSpec((1, t, 1), lambda i, j, *_: (i, j, 0))
    grid_spec = pltpu.PrefetchScalarGridSpec(
        num_scalar_prefetch=2,
        grid=(b, nt),
        in_specs=[pl.BlockSpec(memory_space=pl.ANY), col, col,
                  pl.BlockSpec((1, t, d), lambda i, j, *_: (i, j, 0)),
                  pl.BlockSpec((1, 1, d), lambda i, j, *_: (i, 0, 0)),
                  pl.BlockSpec((1, d), lambda i, j, *_: (0, 0))],
        out_specs=pl.BlockSpec((1, t, d), lambda i, j, *_: (i, j, 0)),
        scratch_shapes=[pltpu.VMEM((kc, d), F32), pltpu.VMEM((t, d), F32), pltpu.SemaphoreType.DMA],
    )
    return pl.pallas_call(
        _combine_kernel,
        out_shape=jax.ShapeDtypeStruct((b, n, d), F32),
        grid_spec=grid_spec,
        compiler_params=_params("arbitrary", "arbitrary"),
        name="combine",
    )(r0, r1, z, start_col, cnt_col, x1, g2, final_g.reshape(1, d))


def kernel(x, c, ctx, c_ctx, w_ada, b_ada, norm1_g, w_in, w_a2_f, b_a2_f, w_a2_b, b_a2_b,
           gla_norm_g, w_out, norm2_g, w_router, w_e_gate, w_e_up, w_e_down, final_norm_g):
    b, n, d = x.shape
    depth = w_ada.shape[0]
    assert depth == 1, "the context stream update of non-final layers is not implemented"
    assert n % (LANES * SUB) == 0 and d % LANES == 0
    hk, hv = GLA_HEADS * GLA_DK, GLA_HEADS * GLA_DV
    nk = n // LANES
    cap = CAPACITY_FACTOR * n // N_EXPERTS
    lyr = 0

    cond = jnp.zeros((SUB, d), F32).at[:b].set(c).at[b].set(c_ctx)
    ada = _ada(cond, w_ada[lyr], b_ada[lyr])
    sh1, sc1, g1, sh2, sc2, g2 = [ada[:b, j * d:(j + 1) * d].reshape(b, 1, d) for j in range(6)]
    sh1_c, sc1_c = [jnp.broadcast_to(ada[b, j * d:(j + 1) * d], (b, 1, d)) for j in range(2)]

    w = w_in[lyr]
    s_main = 2 * hk + 2 * hv
    w_main = w[:, :s_main].astype(BF16)
    w_z = jnp.pad(w[:, s_main:s_main + 2 * GATE_RANK], ((0, 0), (0, LANES - 2 * GATE_RANK))).astype(BF16)
    w_u = w[:, s_main + 2 * GATE_RANK:].astype(BF16)
    main_c, _, z_c = _inproj(ctx, norm1_g[lyr], sh1_c, sc1_c, w_main, w_u, w_z, tm=ctx.shape[1])
    main_l, u_l, z_l = _inproj(x, norm1_g[lyr], sh1, sc1, w_main, w_u, w_z, tm=512)

    zero = jnp.zeros((b, GLA_HEADS, GLA_DK, GLA_DV), F32)
    _, s_f = _gla(main_c, z_c, w_a2_f[lyr], b_a2_f[lyr], zero, reverse=False, r=ctx.shape[1])
    _, s_b = _gla(main_c, z_c, w_a2_b[lyr], b_a2_b[lyr], zero, reverse=True, r=ctx.shape[1])
    o_f, _ = _gla(main_l, z_l, w_a2_f[lyr], b_a2_f[lyr], s_f, reverse=False, r=256)
    o_b, _ = _gla(main_l, z_l, w_a2_b[lyr], b_a2_b[lyr], s_b, reverse=True, r=256)

    w1, w2, csm = _dft_tables(n)
    yf = _four_b(_four_a(u_l.reshape(b, nk, LANES, u_l.shape[-1]), w1), w2, csm).reshape(b, n, -1)

    x1, hf, aff_t = _mix(o_f, o_b, main_l, yf, x, gla_norm_g[lyr], w_out[lyr].astype(BF16), g1,
                         norm2_g[lyr], sh2, sc2, w_router[lyr].T, tm=256)

    a_tab = aff_t.reshape(b, N_EXPERTS, nk, LANES).transpose(0, 3, 1, 2).reshape(b, LANES, N_EXPERTS * nk)
    cum, pos, start, cnt = _route_a(a_tab, nk, cap)
    tok, gate, dst = _route_b(cum, a_tab, pos, nk, cap)

    z = _expert(tok, dst, hf, gate.reshape(b, N_EXPERTS, cap, 1), w_e_gate[lyr].astype(BF16),
                w_e_up[lyr].astype(BF16), w_e_down[lyr].astype(BF16), m=256)

    t_tile = 256
    start_tok = start.transpose(0, 2, 1).reshape(b, n)
    cnt_tok = cnt.transpose(0, 2, 1).reshape(b, n)
    r0 = start_tok[:, ::t_tile].astype(I32)
    r1 = jnp.concatenate([r0[:, 1:], jnp.full((b, 1), N_EXPERTS * cap, I32)], axis=1)
    return _combine(r0.reshape(-1), r1.reshape(-1), z, start_tok.reshape(b, n, 1), cnt_tok.reshape(b, n, 1),
                    x1, g2, final_norm_g, t=t_tile, kc=256)
```

```python
import functools

import jax
import jax.numpy as jnp
import numpy as np
from jax import lax
from jax.experimental import pallas as pl
from jax.experimental.pallas import tpu as pltpu

F32 = jnp.float32
BF16 = jnp.bfloat16
I32 = jnp.int32
HIGHEST = lax.Precision.HIGHEST

EPS = 1e-6
GLA_HEADS = 4
GLA_DK = 128
GLA_DV = 256
GATE_RANK = 16
GATE_NORMALIZER = 16.0
FOURIER_GROUP_DIM = 128
N_EXPERTS = 16
CAPACITY_FACTOR = 2
LANES = 128
VMEM_LIMIT = 56 * 1024 * 1024


def _params(*sem):
    return pltpu.CompilerParams(dimension_semantics=sem, vmem_limit_bytes=VMEM_LIMIT)


def _resident(block_shape, index_map):
    return pl.BlockSpec(block_shape, index_map, pipeline_mode=pl.Buffered(1))


def _ada_kernel(c_ref, w_ref, b_ref, o_ref):
    c = c_ref[...]
    s = c * jax.nn.sigmoid(c)
    o_ref[...] = jnp.dot(s.astype(BF16), w_ref[...].astype(BF16),
                         preferred_element_type=F32) + b_ref[...]


def _ada(cond, w_ada, b_ada):
    rows, d = cond.shape
    n = w_ada.shape[1]
    tn = 1024
    return pl.pallas_call(
        _ada_kernel,
        out_shape=jax.ShapeDtypeStruct((rows, n), F32),
        grid=(n // tn,),
        in_specs=[pl.BlockSpec((rows, d), lambda j: (0, 0)),
                  pl.BlockSpec((d, tn), lambda j: (0, j)),
                  pl.BlockSpec((1, tn), lambda j: (0, j))],
        out_specs=pl.BlockSpec((rows, tn), lambda j: (0, j)),
        compiler_params=_params("arbitrary"),
        name="ada",
    )(cond, w_ada, b_ada.reshape(1, n))


def _rms_mod(x, g, shift, scale):
    y = x * lax.rsqrt(jnp.mean(x * x, axis=-1, keepdims=True) + EPS)
    return (y * g) * (1.0 + scale) + shift


def _inproj_kernel(x_ref, g_ref, sh_ref, sc_ref, wm_ref, wu_ref, wz_ref, om_ref, ou_ref, oz_ref):
    h = _rms_mod(x_ref[0], g_ref[...], sh_ref[0], sc_ref[0]).astype(BF16)
    n = wm_ref.shape[1]
    step = 1024
    for j in range(n // step):
        om_ref[0, :, j * step:(j + 1) * step] = jnp.dot(
            h, wm_ref[:, j * step:(j + 1) * step], preferred_element_type=F32).astype(BF16)
    ou_ref[0] = jnp.dot(h, wu_ref[...], preferred_element_type=F32)
    oz_ref[0] = jnp.dot(h, wz_ref[...], preferred_element_type=F32)


def _inproj(x, norm_g, shift, scale, w_main, w_u, w_z, tm):
    b, n, d = x.shape
    nm = w_main.shape[1]
    nu = w_u.shape[1]
    return pl.pallas_call(
        _inproj_kernel,
        out_shape=(jax.ShapeDtypeStruct((b, n, nm), BF16),
                   jax.ShapeDtypeStruct((b, n, nu), F32),
                   jax.ShapeDtypeStruct((b, n, LANES), F32)),
        grid=(b, n // tm),
        in_specs=[pl.BlockSpec((1, tm, d), lambda i, j: (i, j, 0)),
                  _resident((1, d), lambda i, j: (0, 0)),
                  pl.BlockSpec((1, 1, d), lambda i, j: (i, 0, 0)),
                  pl.BlockSpec((1, 1, d), lambda i, j: (i, 0, 0)),
                  _resident((d, nm), lambda i, j: (0, 0)),
                  _resident((d, nu), lambda i, j: (0, 0)),
                  _resident((d, LANES), lambda i, j: (0, 0))],
        out_specs=(pl.BlockSpec((1, tm, nm), lambda i, j: (i, j, 0)),
                   pl.BlockSpec((1, tm, nu), lambda i, j: (i, j, 0)),
                   pl.BlockSpec((1, tm, LANES), lambda i, j: (i, j, 0))),
        compiler_params=_params("parallel", "arbitrary"),
        name="inproj",
    )(x, norm_g.reshape(1, d), shift, scale, w_main, w_u, w_z)


def _gla_kernel(q_ref, k_ref, v_ref, z_ref, wa_ref, ba_ref, s0_ref, o_ref, sf_ref, st_ref, *,
                reverse, zoff):
    i = pl.program_id(1)

    @pl.when(i == 0)
    def _():
        st_ref[...] = s0_ref[0]

    r = q_ref.shape[1]
    z = z_ref[0][:, zoff:zoff + GATE_RANK]
    pre = jnp.dot(z, wa_ref[...], preferred_element_type=F32, precision=HIGHEST) + ba_ref[...]
    la = jax.nn.log_sigmoid(pre) * (1.0 / GATE_NORMALIZER)
    row = lax.broadcasted_iota(I32, (r, r), 0)
    col = lax.broadcasted_iota(I32, (r, r), 1)
    keep = (col >= row) if reverse else (col <= row)
    c = jnp.dot(keep.astype(F32), la, preferred_element_type=F32, precision=HIGHEST)
    tot = c[0:1, :] if reverse else c[r - 1:r, :]
    ref = 0.5 * tot
    e_in = jnp.exp(c)
    e_q = jnp.exp(c - ref)
    e_k = jnp.exp(ref - c)
    e_end = jnp.exp(tot - c)
    e_tot = jnp.exp(tot)
    q = q_ref[0].astype(F32) * (GLA_DK ** -0.5)
    k = k_ref[0].astype(F32)
    q_in = (q * e_in).astype(BF16)
    q_d = (q * e_q).astype(BF16)
    k_i = (k * e_k).astype(BF16)
    k_e = (k * e_end).astype(BF16)
    v = v_ref[0]
    eye = lax.broadcasted_iota(I32, (GLA_DK, GLA_DK), 0) == lax.broadcasted_iota(I32, (GLA_DK, GLA_DK), 1)
    for h in range(GLA_HEADS):
        ks = slice(h * GLA_DK, (h + 1) * GLA_DK)
        vs = slice(h * GLA_DV, (h + 1) * GLA_DV)
        s_in = st_ref[h]
        sc = lax.dot_general(q_d[:, ks], k_i[:, ks], (((1,), (1,)), ((), ())),
                             preferred_element_type=F32)
        sc = jnp.where(keep, sc, 0.0).astype(BF16)
        o = jnp.dot(sc, v[:, vs], preferred_element_type=F32)
        o = o + jnp.dot(q_in[:, ks], s_in.astype(BF16), preferred_element_type=F32)
        o_ref[0, :, vs] = o
        kv = lax.dot_general(k_e[:, ks], v[:, vs], (((0,), (0,)), ((), ())),
                             preferred_element_type=F32)
        dcol = jnp.sum(jnp.where(eye, e_tot[:, ks], 0.0), axis=1, keepdims=True)
        st_ref[h] = dcol * s_in + kv

    @pl.when(i == pl.num_programs(1) - 1)
    def _():
        sf_ref[0] = st_ref[...]


def _gla(main, z, wa, ba, s0, *, reverse, r):
    b, n, _ = main.shape
    nb = n // r
    hk = GLA_HEADS * GLA_DK
    hv = GLA_HEADS * GLA_DV
    pos = (lambda j: nb - 1 - j) if reverse else (lambda j: j)
    kern = functools.partial(_gla_kernel, reverse=reverse, zoff=GATE_RANK if reverse else 0)
    return pl.pallas_call(
        kern,
        out_shape=(jax.ShapeDtypeStruct((b, n, hv), F32),
                   jax.ShapeDtypeStruct((b, GLA_HEADS, GLA_DK, GLA_DV), F32)),
        grid=(b, nb),
        in_specs=[pl.BlockSpec((1, r, hk), lambda i, j: (i, pos(j), 0)),
                  pl.BlockSpec((1, r, hk), lambda i, j: (i, pos(j), 1)),
                  pl.BlockSpec((1, r, hv), lambda i, j: (i, pos(j), 1)),
                  pl.BlockSpec((1, r, LANES), lambda i, j: (i, pos(j), 0)),
                  _resident((GATE_RANK, hk), lambda i, j: (0, 0)),
                  _resident((1, hk), lambda i, j: (0, 0)),
                  pl.BlockSpec((1, GLA_HEADS, GLA_DK, GLA_DV), lambda i, j: (i, 0, 0, 0))],
        out_specs=(pl.BlockSpec((1, r, hv), lambda i, j: (i, pos(j), 0)),
                   pl.BlockSpec((1, GLA_HEADS, GLA_DK, GLA_DV), lambda i, j: (i, 0, 0, 0))),
        scratch_shapes=[pltpu.VMEM((GLA_HEADS, GLA_DK, GLA_DV), F32)],
        compiler_params=_params("parallel", "arbitrary"),
        name="gla_bwd" if reverse else "gla_fwd",
    )(main, main, main, z, wa, ba.reshape(1, hk), s0)


SUB = 8


def _four_a_kernel(x_ref, w_ref, o_ref):
    _, n1, sub, ch = x_ref.shape
    x = x_ref[0].reshape(n1 * sub, ch).astype(BF16)
    a = jnp.dot(w_ref[...], x, preferred_element_type=F32)
    o_ref[0] = a.reshape(2, n1, sub, ch)


def _four_a(u4, w1s):
    b, n1, n2, ch = u4.shape
    return pl.pallas_call(
        _four_a_kernel,
        out_shape=jax.ShapeDtypeStruct((b, 2, n1, n2, ch), F32),
        grid=(b, n2 // SUB),
        in_specs=[pl.BlockSpec((1, n1, SUB, ch), lambda i, j: (i, 0, j, 0)),
                  _resident(w1s.shape, lambda i, j: (0, 0))],
        out_specs=pl.BlockSpec((1, 2, n1, SUB, ch), lambda i, j: (i, 0, 0, j, 0)),
        compiler_params=_params("parallel", "arbitrary"),
        name="four_a",
    )(u4, w1s)


def _four_b_kernel(a_ref, w_ref, cs_ref, o_ref):
    ch = o_ref.shape[-1]
    ng = ch // FOURIER_GROUP_DIM
    cs = cs_ref[...]
    for j in range(SUB):
        a = jnp.concatenate([a_ref[0, 0, j], a_ref[0, 1, j]], axis=0).astype(BF16)
        a2 = jnp.dot(w_ref[j], a, preferred_element_type=F32).astype(BF16)
        n2 = a2.shape[0] // 2
        lhs = jnp.concatenate(
            [jnp.concatenate([a2[:n2, g * LANES:(g + 1) * LANES], a2[n2:, g * LANES:(g + 1) * LANES]], axis=1)
             for g in range(ng)], axis=0)
        y = jnp.dot(lhs, cs, preferred_element_type=F32)
        y = jnp.concatenate([y[g * n2:(g + 1) * n2] for g in range(ng)], axis=1)
        o_ref[0, :, j, :] = y


def _four_b(a, w2, cs):
    b, _, n1, n2, ch = a.shape
    return pl.pallas_call(
        _four_b_kernel,
        out_shape=jax.ShapeDtypeStruct((b, n2, n1, ch), F32),
        grid=(b, n1 // SUB),
        in_specs=[pl.BlockSpec((1, 2, SUB, n2, ch), lambda i, j: (i, 0, j, 0, 0)),
                  pl.BlockSpec((SUB, 2 * n2, 2 * n2), lambda i, j: (j, 0, 0)),
                  _resident((2 * FOURIER_GROUP_DIM, FOURIER_GROUP_DIM), lambda i, j: (0, 0))],
        out_specs=pl.BlockSpec((1, n2, SUB, ch), lambda i, j: (i, 0, j, 0)),
        compiler_params=_params("parallel", "arbitrary"),
        name="four_b",
    )(a, w2, cs)


def _dft_tables(n):
    n1 = n // LANES

    def cs(num, den, count):
        ang = (num % den).astype(F32) * F32(2.0 * np.pi / den)
        scale = float(count) ** -0.5
        return jnp.cos(ang) * scale, jnp.sin(ang) * scale

    i1 = jnp.arange(n1, dtype=I32)
    il = jnp.arange(LANES, dtype=I32)
    c1, s1 = cs(i1[:, None] * i1[None, :], n1, n1)
    w1 = jnp.stack([c1, -s1], axis=0)
    w1 = jnp.einsum("rkt,jJ->rkjtJ", w1, jnp.eye(SUB, dtype=F32))
    w1 = w1.reshape(2 * n1 * SUB, n1 * SUB).astype(BF16)
    kk = i1[:, None, None] + n1 * il[None, :, None]
    c2, s2 = cs(kk * il[None, None, :], n, LANES)
    w2 = jnp.concatenate([jnp.concatenate([c2, s2], axis=2),
                          jnp.concatenate([-s2, c2], axis=2)], axis=1).astype(BF16)
    cc, sc = cs(il[:, None] * il[None, :], FOURIER_GROUP_DIM, FOURIER_GROUP_DIM)
    csm = jnp.concatenate([cc, sc], axis=0).astype(BF16)
    return w1, w2, csm


def _mix_kernel(of_ref, ob_ref, g_ref, yf_ref, x_ref, gn_ref, wo_ref, g1_ref, n2_ref, sh_ref, sc_ref,
                wr_ref, x1_ref, hf_ref, aff_ref):
    tm = x_ref.shape[1]
    sub = 256 if tm % 256 == 0 else tm
    for r0 in range(0, tm, sub):
        rows = slice(r0, r0 + sub)
        o = of_ref[0, rows, :] + ob_ref[0, rows, :]
        heads = []
        for h in range(GLA_HEADS):
            oh = o[:, h * GLA_DV:(h + 1) * GLA_DV]
            heads.append(oh * lax.rsqrt(jnp.mean(oh * oh, axis=-1, keepdims=True) + EPS))
        on = jnp.concatenate(heads, axis=1) * gn_ref[...]
        g = g_ref[0, rows, :].astype(F32)
        y_gla = (on * (g * jax.nn.sigmoid(g))).astype(BF16)
        hw = y_gla.shape[1]
        acc = jnp.dot(y_gla, wo_ref[:hw, :], preferred_element_type=F32)
        acc = acc + jnp.dot(yf_ref[0, rows, :].astype(BF16), wo_ref[hw:, :], preferred_element_type=F32)
        x1 = x_ref[0, rows, :] + g1_ref[0] * acc
        x1_ref[0, rows, :] = x1
        hf = _rms_mod(x1, n2_ref[...], sh_ref[0], sc_ref[0])
        hf_ref[0, rows, :] = hf
        logits = lax.dot_general(wr_ref[...], hf, (((1,), (1,)), ((), ())),
                                 preferred_element_type=F32, precision=HIGHEST)
        p = jnp.exp(logits - jnp.max(logits, axis=0, keepdims=True))
        aff_ref[0, :, rows] = p / jnp.sum(p, axis=0, keepdims=True)


def _mix(o_f, o_b, main, yf, x, gla_norm_g, w_out, g1, norm2_g, sh2, sc2, w_router_t, tm):
    b, n, d = x.shape
    hv = o_f.shape[-1]
    ne = w_router_t.shape[0]
    row = lambda w: pl.BlockSpec((1, tm, w), lambda i, j: (i, j, 0))
    vec = pl.BlockSpec((1, 1, d), lambda i, j: (i, 0, 0))
    return pl.pallas_call(
        _mix_kernel,
        out_shape=(jax.ShapeDtypeStruct((b, n, d), F32),
                   jax.ShapeDtypeStruct((b, n, d), F32),
                   jax.ShapeDtypeStruct((b, ne, n), F32)),
        grid=(b, n // tm),
        in_specs=[row(hv), row(hv),
                  pl.BlockSpec((1, tm, hv), lambda i, j: (i, j, 2)),
                  row(yf.shape[-1]), row(d),
                  _resident((1, hv), lambda i, j: (0, 0)),
                  _resident(w_out.shape, lambda i, j: (0, 0)),
                  vec,
                  _resident((1, d), lambda i, j: (0, 0)),
                  vec, vec,
                  _resident((ne, d), lambda i, j: (0, 0))],
        out_specs=(row(d), row(d), pl.BlockSpec((1, ne, tm), lambda i, j: (i, 0, j))),
        compiler_params=_params("parallel", "arbitrary"),
        name="mix",
    )(o_f, o_b, main, yf, x, gla_norm_g.reshape(1, hv), w_out, g1, norm2_g.reshape(1, d), sh2, sc2,
      w_router_t)


def _per_expert_total(v, nk):
    ne = v.shape[1] // nk
    rows = jnp.concatenate([v[:, e * nk:(e + 1) * nk] for e in range(ne)], axis=0)
    tot = jnp.broadcast_to(jnp.sum(rows, axis=1, keepdims=True), rows.shape)
    return jnp.concatenate([tot[e:e + 1, :] for e in range(ne)], axis=1)


def _token_cumsum(m, nk):
    ne = m.shape[1] // nk
    tril = (lax.broadcasted_iota(I32, (LANES, LANES), 1) <= lax.broadcasted_iota(I32, (LANES, LANES), 0))
    within = jnp.dot(tril.astype(BF16), m.astype(BF16), preferred_element_type=F32)
    coltot = within[LANES - 1:LANES, :]
    rows = jnp.concatenate([coltot[:, e * nk:(e + 1) * nk] for e in range(ne)], axis=0)
    upper = (lax.broadcasted_iota(I32, (nk, nk), 0) < lax.broadcasted_iota(I32, (nk, nk), 1))
    offs = jnp.dot(rows.astype(BF16), upper.astype(BF16), preferred_element_type=F32)
    offs = jnp.concatenate([offs[e:e + 1, :] for e in range(ne)], axis=1)
    return within + offs


def _route_a_kernel(a_ref, cum_ref, pos_ref, start_ref, cnt_ref, *, nk, cap):
    a = a_ref[0]
    ne = a.shape[1] // nk

    def key_value(key):
        return pltpu.bitcast(jnp.where(key < 0, key ^ jnp.int32(0x7FFFFFFF), key), F32)

    def count_ge(key):
        colsum = jnp.sum(jnp.where(a >= key_value(key), 1.0, 0.0), axis=0, keepdims=True)
        return _per_expert_total(colsum, nk)

    fcap = jnp.float32(cap)
    int_min = jnp.full((1, a.shape[1]), -2 ** 31, I32)
    thr = jnp.where(count_ge(jnp.zeros_like(int_min)) >= fcap, 0, int_min)

    def step(i, thr):
        cand = thr + jnp.left_shift(jnp.int32(1), 30 - i)
        return jnp.where(count_ge(cand) >= fcap, cand, thr)

    thr = lax.fori_loop(0, 31, step, thr)
    gt = a >= key_value(thr + 1)
    eq = (a >= key_value(thr)) & jnp.logical_not(gt)
    n_gt = _per_expert_total(jnp.sum(jnp.where(gt, 1.0, 0.0), axis=0, keepdims=True), nk)
    eqf = jnp.where(eq, 1.0, 0.0)
    tie_rank = _token_cumsum(eqf, nk) - eqf
    sel = jnp.where(gt | (eq & (tie_rank < fcap - n_gt)), 1.0, 0.0)
    cum = _token_cumsum(sel, nk)
    cum_ref[0] = cum
    excl = cum - sel
    start = excl[:, 0:nk]
    cnt = sel[:, 0:nk]
    for e in range(1, ne):
        start = start + excl[:, e * nk:(e + 1) * nk]
        cnt = cnt + sel[:, e * nk:(e + 1) * nk]
    start_ref[0] = start
    cnt_ref[0] = cnt
    before = jnp.zeros_like(start)
    for e in range(ne):
        pos_ref[0, :, e * nk:(e + 1) * nk] = start + before
        before = before + sel[:, e * nk:(e + 1) * nk]


def _route_a(a_tab, nk, cap):
    b, _, w = a_tab.shape
    tab = lambda width: pl.BlockSpec((1, LANES, width), lambda i: (i, 0, 0))
    return pl.pallas_call(
        functools.partial(_route_a_kernel, nk=nk, cap=cap),
        out_shape=(jax.ShapeDtypeStruct((b, LANES, w), F32), jax.ShapeDtypeStruct((b, LANES, w), F32),
                   jax.ShapeDtypeStruct((b, LANES, nk), F32), jax.ShapeDtypeStruct((b, LANES, nk), F32)),
        grid=(b,),
        in_specs=[tab(w)],
        out_specs=(tab(w), tab(w), tab(nk), tab(nk)),
        compiler_params=_params("parallel"),
        name="route_a",
    )(a_tab)


def _route_b_kernel(cum_ref, a_ref, pos_ref, tok_ref, gate_ref, dst_ref, *, cap):
    cum = cum_ref[0]
    nk = cum.shape[1]
    col_end = cum[LANES - 1:LANES, :]
    eye = lax.broadcasted_iota(I32, (nk, nk), 0) == lax.broadcasted_iota(I32, (nk, nk), 1)
    col_end_c = jnp.sum(jnp.where(eye, col_end, 0.0), axis=1, keepdims=True)
    s = lax.broadcasted_iota(I32, (1, cap), 1).astype(F32)
    blk = jnp.sum(jnp.where(col_end_c <= s, 1.0, 0.0), axis=0, keepdims=True)
    pick_col = jnp.where(lax.broadcasted_iota(I32, (nk, cap), 0).astype(F32) == blk, 1.0, 0.0)
    gather = lambda tab: jnp.dot(tab, pick_col, preferred_element_type=F32, precision=HIGHEST)
    cum_rows = jnp.round(gather(cum))
    lpos = jnp.sum(jnp.where(cum_rows <= s, 1.0, 0.0), axis=0, keepdims=True)
    pick_row = lax.broadcasted_iota(I32, (LANES, cap), 0).astype(F32) == lpos
    tok_ref[0, 0] = (blk * LANES + lpos).astype(I32)
    gate_ref[0, 0] = jnp.sum(jnp.where(pick_row, gather(a_ref[0]), 0.0), axis=0, keepdims=True)
    dst = jnp.sum(jnp.where(pick_row, gather(pos_ref[0]), 0.0), axis=0, keepdims=True)
    dst_ref[0, 0] = jnp.round(dst).astype(I32)


def _route_b(cum, a_tab, pos, nk, cap):
    b, _, w = cum.shape
    ne = w // nk
    tab = pl.BlockSpec((1, LANES, nk), lambda i, e: (i, 0, e))
    slot = pl.BlockSpec((1, 1, 1, cap), lambda i, e: (i, e, 0, 0))
    sds = lambda dt: jax.ShapeDtypeStruct((b, ne, 1, cap), dt)
    return pl.pallas_call(
        functools.partial(_route_b_kernel, cap=cap),
        out_shape=(sds(I32), sds(F32), sds(I32)),
        grid=(b, ne),
        in_specs=[tab, tab, tab],
        out_specs=(slot, slot, slot),
        compiler_params=_params("parallel", "arbitrary"),
        name="route_b",
    )(cum, a_tab, pos)


def _expert_kernel(tok_ref, tokn_ref, dstp_ref, dst_ref, hf_ref, gate_ref, wg_ref, wu_ref, wd_ref, z_ref,
                   xbuf, ybuf, gsem, ssem):
    nb, nm = pl.num_programs(1), pl.num_programs(2)
    total = pl.num_programs(0) * nb * nm
    b, j = pl.program_id(1), pl.program_id(2)
    s = (pl.program_id(0) * nb + b) * nm + j
    m = xbuf.shape[1]
    slot = s % 2
    other = 1 - slot
    s_next = jnp.minimum(s + 1, total - 1)
    s_prev = jnp.maximum(s - 1, 0)

    def start_gather(idx_ref, step, buf, rows):
        bb, base = (step // nm) % nb, (step % nm) * m
        for i in rows:
            t = idx_ref[0, 0, base + i]
            pltpu.make_async_copy(hf_ref.at[bb, pl.ds(t, 1), :], xbuf.at[buf, pl.ds(i, 1), :],
                                  gsem.at[buf]).start()

    def start_scatter(idx_ref, step, buf, rows):
        bb, base = (step // nm) % nb, (step % nm) * m
        for i in rows:
            p = idx_ref[0, 0, base + i]
            pltpu.make_async_copy(ybuf.at[buf, pl.ds(i, 1), :], z_ref.at[bb, pl.ds(p, 1), :],
                                  ssem.at[buf]).start()

    def wait_gather(buf):
        pltpu.make_async_copy(hf_ref.at[0, pl.ds(0, m), :], xbuf.at[buf], gsem.at[buf]).wait()

    def wait_scatter(buf):
        pltpu.make_async_copy(ybuf.at[buf], z_ref.at[0, pl.ds(0, m), :], ssem.at[buf]).wait()

    @pl.when(s == 0)
    def _():
        start_gather(tok_ref, s, slot, range(m))

    @pl.when(s >= 2)
    def _():
        wait_scatter(slot)

    wait_gather(slot)

    def compute(scatter_prev):
        half = m // 2
        start_gather(tokn_ref, s_next, other, range(0, half))
        x = xbuf[slot].astype(BF16)
        h1 = jnp.dot(x, wg_ref[0], preferred_element_type=F32)
        start_gather(tokn_ref, s_next, other, range(half, m))
        h2 = jnp.dot(x, wu_ref[0], preferred_element_type=F32)
        hid = (h1 * jax.nn.sigmoid(h1) * h2).astype(BF16)
        if scatter_prev:
            start_scatter(dstp_ref, s_prev, other, range(m))
        ybuf[slot] = jnp.dot(hid, wd_ref[0], preferred_element_type=F32) * gate_ref[0, 0]

    @pl.when(s == 0)
    def _():
        compute(False)

    @pl.when(s > 0)
    def _():
        compute(True)

    @pl.when(s == total - 1)
    def _():
        start_scatter(dst_ref, s, slot, range(m))
        wait_gather(other)

        @pl.when(s > 0)
        def _():
            wait_scatter(other)

        wait_scatter(slot)


def _expert(tok, dst, hf, gate_col, w_gate, w_up, w_down, m):
    b, n, d = hf.shape
    ne, _, f = w_gate.shape
    cap = tok.shape[-1]
    nm = cap // m
    total = ne * b * nm

    def idx(delta):
        def index_map(e, i, j):
            s = jnp.clip((e * b + i) * nm + j + delta, 0, total - 1)
            return (((s // nm) % b) * ne + s // (b * nm), 0, 0)
        return pl.BlockSpec((1, 1, cap), index_map, memory_space=pltpu.SMEM)

    tok3, dst3 = tok.reshape(b * ne, 1, cap), dst.reshape(b * ne, 1, cap)
    return pl.pallas_call(
        _expert_kernel,
        out_shape=jax.ShapeDtypeStruct((b, ne * cap, d), F32),
        grid=(ne, b, nm),
        in_specs=[idx(0), idx(1), idx(-1), idx(0),
                  pl.BlockSpec(memory_space=pl.ANY),
                  pl.BlockSpec((1, 1, m, 1), lambda e, i, j: (i, e, j, 0)),
                  pl.BlockSpec((1, d, f), lambda e, i, j: (e, 0, 0)),
                  pl.BlockSpec((1, d, f), lambda e, i, j: (e, 0, 0)),
                  pl.BlockSpec((1, f, d), lambda e, i, j: (e, 0, 0))],
        out_specs=pl.BlockSpec(memory_space=pl.ANY),
        scratch_shapes=[pltpu.VMEM((2, m, d), F32), pltpu.VMEM((2, m, d), F32),
                        pltpu.SemaphoreType.DMA((2,)), pltpu.SemaphoreType.DMA((2,))],
        compiler_params=_params("arbitrary", "arbitrary", "arbitrary"),
        name="expert",
    )(tok3, tok3, dst3, dst3, hf, gate_col, w_gate, w_up, w_down)


def _combine_kernel(r0_ref, r1_ref, z_ref, st_ref, cn_ref, x1_ref, g2_ref, fg_ref, o_ref,
                    zwin, zbuf, acc, wsem, sem):
    nt = pl.num_programs(1)
    total = pl.num_programs(0) * nt
    idx = pl.program_id(0) * nt + pl.program_id(1)
    slot = idx % 2
    win, kc = zwin.shape[1], zbuf.shape[0]
    zrows = z_ref.shape[1]

    def window(k):
        r0a = (r0_ref[k] // SUB) * SUB
        return r0a, pl.multiple_of(jnp.minimum(r0a, zrows - win), SUB)

    def window_copy(k, buf):
        _, rb = window(k)
        return pltpu.make_async_copy(z_ref.at[k // nt, pl.ds(rb, win), :], zwin.at[buf], wsem.at[buf])

    @pl.when(idx == 0)
    def _():
        window_copy(idx, slot).start()

    @pl.when(idx + 1 < total)
    def _():
        window_copy(idx + 1, 1 - slot).start()

    r0a, rb = window(idx)
    r1 = r1_ref[idx]
    start = st_ref[0]
    end = start + cn_ref[0]

    def one_hot(first_row, nominal, width):
        rid = lax.broadcasted_iota(I32, (1, width), 1) + first_row
        ridf = rid.astype(F32)
        mine = (ridf >= start) & (ridf < end) & (rid >= nominal)
        return jnp.where(mine, 1.0, 0.0).astype(BF16)

    p = one_hot(rb, r0a, win)
    window_copy(idx, slot).wait()
    acc[...] = jnp.dot(p, zwin[slot].astype(BF16), preferred_element_type=F32)

    def chunk(c, carry):
        nominal = r0a + win + c * kc
        cb = pl.multiple_of(jnp.minimum(nominal, zrows - kc), SUB)
        cp = pltpu.make_async_copy(z_ref.at[pl.program_id(0), pl.ds(cb, kc), :], zbuf, sem)
        cp.start()
        pc = one_hot(cb, nominal, kc)
        cp.wait()
        acc[...] += jnp.dot(pc, zbuf[...].astype(BF16), preferred_element_type=F32)
        return carry

    lax.fori_loop(0, jnp.maximum(r1 - (r0a + win) + kc - 1, 0) // kc, chunk, 0)
    y = x1_ref[0] + g2_ref[0] * acc[...]
    o_ref[0] = y * lax.rsqrt(jnp.mean(y * y, axis=-1, keepdims=True) + EPS) * fg_ref[...]


def _combine(r0, r1, z, start_col, cnt_col, x1, g2, final_g, t, win, kc):
    b, n, d = x1.shape
    nt = n // t
    col = pl.BlockSpec((1, t, 1), lambda i, j, *_: (i, j, 0))
    grid_spec = pltpu.PrefetchScalarGridSpec(
        num_scalar_prefetch=2,
        grid=(b, nt),
        in_specs=[pl.BlockSpec(memory_space=pl.ANY), col, col,
                  pl.BlockSpec((1, t, d), lambda i, j, *_: (i, j, 0)),
                  pl.BlockSpec((1, 1, d), lambda i, j, *_: (i, 0, 0)),
                  pl.BlockSpec((1, d), lambda i, j, *_: (0, 0))],
        out_specs=pl.BlockSpec((1, t, d), lambda i, j, *_: (i, j, 0)),
        scratch_shapes=[pltpu.VMEM((2, win, d), F32), pltpu.VMEM((kc, d), F32), pltpu.VMEM((t, d), F32),
                        pltpu.SemaphoreType.DMA((2,)), pltpu.SemaphoreType.DMA],
    )
    return pl.pallas_call(
        _combine_kernel,
        out_shape=jax.ShapeDtypeStruct((b, n, d), F32),
        grid_spec=grid_spec,
        compiler_params=_params("arbitrary", "arbitrary"),
        name="combine",
    )(r0, r1, z, start_col, cnt_col, x1, g2, final_g.reshape(1, d))


def kernel(x, c, ctx, c_ctx, w_ada, b_ada, norm1_g, w_in, w_a2_f, b_a2_f, w_a2_b, b_a2_b,
           gla_norm_g, w_out, norm2_g, w_router, w_e_gate, w_e_up, w_e_down, final_norm_g):
    b, n, d = x.shape
    depth = w_ada.shape[0]
    assert depth == 1, "the context stream update of non-final layers is not implemented"
    assert n % (LANES * SUB) == 0 and d % LANES == 0
    hk, hv = GLA_HEADS * GLA_DK, GLA_HEADS * GLA_DV
    nk = n // LANES
    cap = CAPACITY_FACTOR * n // N_EXPERTS
    lyr = 0

    cond = jnp.zeros((SUB, d), F32).at[:b].set(c).at[b].set(c_ctx)
    ada = _ada(cond, w_ada[lyr], b_ada[lyr])
    sh1, sc1, g1, sh2, sc2, g2 = [ada[:b, j * d:(j + 1) * d].reshape(b, 1, d) for j in range(6)]
    sh1_c, sc1_c = [jnp.broadcast_to(ada[b, j * d:(j + 1) * d], (b, 1, d)) for j in range(2)]

    w = w_in[lyr]
    s_main = 2 * hk + 2 * hv
    w_main = w[:, :s_main].astype(BF16)
    w_z = jnp.pad(w[:, s_main:s_main + 2 * GATE_RANK], ((0, 0), (0, LANES - 2 * GATE_RANK))).astype(BF16)
    w_u = w[:, s_main + 2 * GATE_RANK:].astype(BF16)
    main_c, _, z_c = _inproj(ctx, norm1_g[lyr], sh1_c, sc1_c, w_main, w_u, w_z, tm=ctx.shape[1])
    main_l, u_l, z_l = _inproj(x, norm1_g[lyr], sh1, sc1, w_main, w_u, w_z, tm=512)

    zero = jnp.zeros((b, GLA_HEADS, GLA_DK, GLA_DV), F32)
    _, s_f = _gla(main_c, z_c, w_a2_f[lyr], b_a2_f[lyr], zero, reverse=False, r=ctx.shape[1])
    _, s_b = _gla(main_c, z_c, w_a2_b[lyr], b_a2_b[lyr], zero, reverse=True, r=ctx.shape[1])
    o_f, _ = _gla(main_l, z_l, w_a2_f[lyr], b_a2_f[lyr], s_f, reverse=False, r=256)
    o_b, _ = _gla(main_l, z_l, w_a2_b[lyr], b_a2_b[lyr], s_b, reverse=True, r=256)

    w1, w2, csm = _dft_tables(n)
    yf = _four_b(_four_a(u_l.reshape(b, nk, LANES, u_l.shape[-1]), w1), w2, csm).reshape(b, n, -1)

    x1, hf, aff_t = _mix(o_f, o_b, main_l, yf, x, gla_norm_g[lyr], w_out[lyr].astype(BF16), g1,
                         norm2_g[lyr], sh2, sc2, w_router[lyr].T, tm=512)

    a_tab = aff_t.reshape(b, N_EXPERTS, nk, LANES).transpose(0, 3, 1, 2).reshape(b, LANES, N_EXPERTS * nk)
    cum, pos, start, cnt = _route_a(a_tab, nk, cap)
    tok, gate, dst = _route_b(cum, a_tab, pos, nk, cap)

    z = _expert(tok, dst, hf, gate.reshape(b, N_EXPERTS, cap, 1), w_e_gate[lyr].astype(BF16),
                w_e_up[lyr].astype(BF16), w_e_down[lyr].astype(BF16), m=256)

    t_tile = 512
    start_tok = start.transpose(0, 2, 1).reshape(b, n)
    cnt_tok = cnt.transpose(0, 2, 1).reshape(b, n)
    r0 = start_tok[:, ::t_tile].astype(I32)
    r1 = jnp.concatenate([r0[:, 1:], jnp.full((b, 1), N_EXPERTS * cap, I32)], axis=1)
    return _combine(r0.reshape(-1), r1.reshape(-1), z, start_tok.reshape(b, n, 1), cnt_tok.reshape(b, n, 1),
                    x1, g2, final_norm_g, t=t_tile, win=5 * t_tile // 2, kc=256)
```

```python
import functools

import jax
import jax.numpy as jnp
import numpy as np
from jax import lax
from jax.experimental import pallas as pl
from jax.experimental.pallas import tpu as pltpu

F32 = jnp.float32
BF16 = jnp.bfloat16
I32 = jnp.int32
HIGHEST = lax.Precision.HIGHEST

EPS = 1e-6
GLA_HEADS = 4
GLA_DK = 128
GLA_DV = 256
GATE_RANK = 16
GATE_NORMALIZER = 16.0
FOURIER_GROUP_DIM = 128
N_EXPERTS = 16
CAPACITY_FACTOR = 2
LANES = 128
VMEM_LIMIT = 56 * 1024 * 1024


def _params(*sem):
    return pltpu.CompilerParams(dimension_semantics=sem, vmem_limit_bytes=VMEM_LIMIT)


def _resident(block_shape, index_map):
    return pl.BlockSpec(block_shape, index_map, pipeline_mode=pl.Buffered(1))


def _ada_kernel(c_ref, w_ref, b_ref, o_ref):
    c = c_ref[...]
    s = c * jax.nn.sigmoid(c)
    o_ref[...] = jnp.dot(s.astype(BF16), w_ref[...].astype(BF16),
                         preferred_element_type=F32) + b_ref[...]


def _ada(cond, w_ada, b_ada):
    rows, d = cond.shape
    n = w_ada.shape[1]
    tn = 1024
    return pl.pallas_call(
        _ada_kernel,
        out_shape=jax.ShapeDtypeStruct((rows, n), F32),
        grid=(n // tn,),
        in_specs=[pl.BlockSpec((rows, d), lambda j: (0, 0)),
                  pl.BlockSpec((d, tn), lambda j: (0, j)),
                  pl.BlockSpec((1, tn), lambda j: (0, j))],
        out_specs=pl.BlockSpec((rows, tn), lambda j: (0, j)),
        compiler_params=_params("arbitrary"),
        name="ada",
    )(cond, w_ada, b_ada.reshape(1, n))


def _rms_mod(x, g, shift, scale):
    y = x * lax.rsqrt(jnp.mean(x * x, axis=-1, keepdims=True) + EPS)
    return (y * g) * (1.0 + scale) + shift


def _inproj_kernel(x_ref, g_ref, sh_ref, sc_ref, wm_ref, wu_ref, wz_ref, om_ref, ou_ref, oz_ref):
    h = _rms_mod(x_ref[0], g_ref[...], sh_ref[0], sc_ref[0]).astype(BF16)
    n = wm_ref.shape[1]
    step = 1024
    for j in range(n // step):
        om_ref[0, :, j * step:(j + 1) * step] = jnp.dot(
            h, wm_ref[:, j * step:(j + 1) * step], preferred_element_type=F32).astype(BF16)
    ou_ref[0] = jnp.dot(h, wu_ref[...], preferred_element_type=F32).reshape(ou_ref.shape[1:])
    oz_ref[0] = jnp.dot(h, wz_ref[...], preferred_element_type=F32)


def _inproj(x, norm_g, shift, scale, w_main, w_u, w_z, tm):
    b, n, d = x.shape
    nm = w_main.shape[1]
    nu = w_u.shape[1]
    return pl.pallas_call(
        _inproj_kernel,
        out_shape=(jax.ShapeDtypeStruct((b, n, nm), BF16),
                   jax.ShapeDtypeStruct((b, n // LANES, LANES, nu), F32),
                   jax.ShapeDtypeStruct((b, n, LANES), F32)),
        grid=(b, n // tm),
        in_specs=[pl.BlockSpec((1, tm, d), lambda i, j: (i, j, 0)),
                  _resident((1, d), lambda i, j: (0, 0)),
                  pl.BlockSpec((1, 1, d), lambda i, j: (i, 0, 0)),
                  pl.BlockSpec((1, 1, d), lambda i, j: (i, 0, 0)),
                  _resident((d, nm), lambda i, j: (0, 0)),
                  _resident((d, nu), lambda i, j: (0, 0)),
                  _resident((d, LANES), lambda i, j: (0, 0))],
        out_specs=(pl.BlockSpec((1, tm, nm), lambda i, j: (i, j, 0)),
                   pl.BlockSpec((1, tm // LANES, LANES, nu), lambda i, j: (i, j, 0, 0)),
                   pl.BlockSpec((1, tm, LANES), lambda i, j: (i, j, 0))),
        compiler_params=_params("parallel", "arbitrary"),
        name="inproj",
    )(x, norm_g.reshape(1, d), shift, scale, w_main, w_u, w_z)


def _gla_kernel(q_ref, k_ref, v_ref, z_ref, wa_ref, ba_ref, s0_ref, o_ref, sf_ref, st_ref, *,
                reverse, zoff):
    i = pl.program_id(1)

    @pl.when(i == 0)
    def _():
        st_ref[...] = s0_ref[0]

    r = q_ref.shape[1]
    z = z_ref[0][:, zoff:zoff + GATE_RANK]
    pre = jnp.dot(z, wa_ref[...], preferred_element_type=F32, precision=HIGHEST) + ba_ref[...]
    la = jax.nn.log_sigmoid(pre) * (1.0 / GATE_NORMALIZER)
    row = lax.broadcasted_iota(I32, (r, r), 0)
    col = lax.broadcasted_iota(I32, (r, r), 1)
    keep = (col >= row) if reverse else (col <= row)
    c = jnp.dot(keep.astype(F32), la, preferred_element_type=F32, precision=HIGHEST)
    tot = c[0:1, :] if reverse else c[r - 1:r, :]
    ref = 0.5 * tot
    e_in = jnp.exp(c)
    e_q = jnp.exp(c - ref)
    e_k = jnp.exp(ref - c)
    e_end = jnp.exp(tot - c)
    e_tot = jnp.exp(tot)
    q = q_ref[0].astype(F32) * (GLA_DK ** -0.5)
    k = k_ref[0].astype(F32)
    q_in = (q * e_in).astype(BF16)
    q_d = (q * e_q).astype(BF16)
    k_i = (k * e_k).astype(BF16)
    k_e = (k * e_end).astype(BF16)
    v = v_ref[0]
    eye = lax.broadcasted_iota(I32, (GLA_DK, GLA_DK), 0) == lax.broadcasted_iota(I32, (GLA_DK, GLA_DK), 1)
    for h in range(GLA_HEADS):
        ks = slice(h * GLA_DK, (h + 1) * GLA_DK)
        vs = slice(h * GLA_DV, (h + 1) * GLA_DV)
        s_in = st_ref[h]
        sc = lax.dot_general(q_d[:, ks], k_i[:, ks], (((1,), (1,)), ((), ())),
                             preferred_element_type=F32)
        sc = jnp.where(keep, sc, 0.0).astype(BF16)
        o = jnp.dot(sc, v[:, vs], preferred_element_type=F32)
        o = o + jnp.dot(q_in[:, ks], s_in.astype(BF16), preferred_element_type=F32)
        o_ref[0, :, vs] = o
        kv = lax.dot_general(k_e[:, ks], v[:, vs], (((0,), (0,)), ((), ())),
                             preferred_element_type=F32)
        dcol = jnp.sum(jnp.where(eye, e_tot[:, ks], 0.0), axis=1, keepdims=True)
        st_ref[h] = dcol * s_in + kv

    @pl.when(i == pl.num_programs(1) - 1)
    def _():
        sf_ref[0] = st_ref[...]


def _gla(main, z, wa, ba, s0, *, reverse, r):
    b, n, _ = main.shape
    nb = n // r
    hk = GLA_HEADS * GLA_DK
    hv = GLA_HEADS * GLA_DV
    pos = (lambda j: nb - 1 - j) if reverse else (lambda j: j)
    kern = functools.partial(_gla_kernel, reverse=reverse, zoff=GATE_RANK if reverse else 0)
    return pl.pallas_call(
        kern,
        out_shape=(jax.ShapeDtypeStruct((b, n, hv), F32),
                   jax.ShapeDtypeStruct((b, GLA_HEADS, GLA_DK, GLA_DV), F32)),
        grid=(b, nb),
        in_specs=[pl.BlockSpec((1, r, hk), lambda i, j: (i, pos(j), 0)),
                  pl.BlockSpec((1, r, hk), lambda i, j: (i, pos(j), 1)),
                  pl.BlockSpec((1, r, hv), lambda i, j: (i, pos(j), 1)),
                  pl.BlockSpec((1, r, LANES), lambda i, j: (i, pos(j), 0)),
                  _resident((GATE_RANK, hk), lambda i, j: (0, 0)),
                  _resident((1, hk), lambda i, j: (0, 0)),
                  pl.BlockSpec((1, GLA_HEADS, GLA_DK, GLA_DV), lambda i, j: (i, 0, 0, 0))],
        out_specs=(pl.BlockSpec((1, r, hv), lambda i, j: (i, pos(j), 0)),
                   pl.BlockSpec((1, GLA_HEADS, GLA_DK, GLA_DV), lambda i, j: (i, 0, 0, 0))),
        scratch_shapes=[pltpu.VMEM((GLA_HEADS, GLA_DK, GLA_DV), F32)],
        compiler_params=_params("parallel", "arbitrary"),
        name="gla_bwd" if reverse else "gla_fwd",
    )(main, main, main, z, wa, ba.reshape(1, hk), s0)


SUB = 8


def _four_a_kernel(x_ref, w_ref, o_ref):
    _, n1, sub, ch = x_ref.shape
    x = x_ref[0].reshape(n1 * sub, ch).astype(BF16)
    a = jnp.dot(w_ref[...], x, preferred_element_type=F32)
    o_ref[0] = a.reshape(2, n1, sub, ch)


def _four_a(u4, w1s):
    b, n1, n2, ch = u4.shape
    return pl.pallas_call(
        _four_a_kernel,
        out_shape=jax.ShapeDtypeStruct((b, 2, n1, n2, ch), F32),
        grid=(b, n2 // SUB),
        in_specs=[pl.BlockSpec((1, n1, SUB, ch), lambda i, j: (i, 0, j, 0)),
                  _resident(w1s.shape, lambda i, j: (0, 0))],
        out_specs=pl.BlockSpec((1, 2, n1, SUB, ch), lambda i, j: (i, 0, 0, j, 0)),
        compiler_params=_params("parallel", "arbitrary"),
        name="four_a",
    )(u4, w1s)


def _four_b_kernel(a_ref, w_ref, cs_ref, o_ref):
    ch = o_ref.shape[-1]
    ng = ch // FOURIER_GROUP_DIM
    cs = cs_ref[...]
    for j in range(SUB):
        a = jnp.concatenate([a_ref[0, 0, j], a_ref[0, 1, j]], axis=0).astype(BF16)
        a2 = jnp.dot(w_ref[j], a, preferred_element_type=F32).astype(BF16)
        n2 = a2.shape[0] // 2
        lhs = jnp.concatenate(
            [jnp.concatenate([a2[:n2, g * LANES:(g + 1) * LANES], a2[n2:, g * LANES:(g + 1) * LANES]], axis=1)
             for g in range(ng)], axis=0)
        y = jnp.dot(lhs, cs, preferred_element_type=F32)
        y = jnp.concatenate([y[g * n2:(g + 1) * n2] for g in range(ng)], axis=1)
        o_ref[0, :, j, :] = y


def _four_b(a, w2, cs):
    b, _, n1, n2, ch = a.shape
    return pl.pallas_call(
        _four_b_kernel,
        out_shape=jax.ShapeDtypeStruct((b, n2, n1, ch), F32),
        grid=(b, n1 // SUB),
        in_specs=[pl.BlockSpec((1, 2, SUB, n2, ch), lambda i, j: (i, 0, j, 0, 0)),
                  pl.BlockSpec((SUB, 2 * n2, 2 * n2), lambda i, j: (j, 0, 0)),
                  _resident((2 * FOURIER_GROUP_DIM, FOURIER_GROUP_DIM), lambda i, j: (0, 0))],
        out_specs=pl.BlockSpec((1, n2, SUB, ch), lambda i, j: (i, 0, j, 0)),
        compiler_params=_params("parallel", "arbitrary"),
        name="four_b",
    )(a, w2, cs)


def _dft_tables(n):
    n1 = n // LANES

    def cs(num, den, count):
        ang = (num % den).astype(F32) * F32(2.0 * np.pi / den)
        scale = float(count) ** -0.5
        return jnp.cos(ang) * scale, jnp.sin(ang) * scale

    i1 = jnp.arange(n1, dtype=I32)
    il = jnp.arange(LANES, dtype=I32)
    c1, s1 = cs(i1[:, None] * i1[None, :], n1, n1)
    w1 = jnp.stack([c1, -s1], axis=0)
    w1 = jnp.einsum("rkt,jJ->rkjtJ", w1, jnp.eye(SUB, dtype=F32))
    w1 = w1.reshape(2 * n1 * SUB, n1 * SUB).astype(BF16)
    kk = i1[:, None, None] + n1 * il[None, :, None]
    c2, s2 = cs(kk * il[None, None, :], n, LANES)
    w2 = jnp.concatenate([jnp.concatenate([c2, s2], axis=2),
                          jnp.concatenate([-s2, c2], axis=2)], axis=1).astype(BF16)
    cc, sc = cs(il[:, None] * il[None, :], FOURIER_GROUP_DIM, FOURIER_GROUP_DIM)
    csm = jnp.concatenate([cc, sc], axis=0).astype(BF16)
    return w1, w2, csm


def _load_rows(ref, lead, nrows, nblk):
    return [ref[lead + (pl.ds(c, nrows, stride=nblk), slice(None))] for c in range(nblk)]


def _store_rows(ref, lead, val, nblk):
    nrows = val.shape[0]
    for c in range(nblk):
        ref[lead + (pl.ds(c, nrows, stride=nblk), slice(None))] = val[:, c * LANES:(c + 1) * LANES]


def _mix_kernel(of_ref, ob_ref, g_ref, yf_ref, x_ref, gn_ref, wo_ref, g1_ref, n2_ref, sh_ref, sc_ref,
                wrh_ref, wrl_ref, x1_ref, aff_ref):
    tm, d = x_ref.shape[1], x_ref.shape[2]
    o = of_ref[0] + ob_ref[0]
    heads = []
    for h in range(GLA_HEADS):
        oh = o[:, h * GLA_DV:(h + 1) * GLA_DV]
        heads.append(oh * lax.rsqrt(jnp.mean(oh * oh, axis=-1, keepdims=True) + EPS))
    on = jnp.concatenate(heads, axis=1) * gn_ref[...]
    g = g_ref[0].astype(F32)
    y_gla = (on * (g * jax.nn.sigmoid(g))).astype(BF16)
    hw = y_gla.shape[1]
    yf = yf_ref[0].reshape(tm, yf_ref.shape[-1]).astype(BF16)
    acc = jnp.dot(y_gla, wo_ref[:hw, :], preferred_element_type=F32)
    acc = acc + jnp.dot(yf, wo_ref[hw:, :], preferred_element_type=F32)
    x1 = x_ref[0] + g1_ref[0] * acc
    _store_rows(x1_ref, (0,), x1, d // LANES)
    hf = _rms_mod(x1, n2_ref[...], sh_ref[0], sc_ref[0])
    hf_hi = hf.astype(BF16)
    hf_lo = (hf - hf_hi.astype(F32)).astype(BF16)
    nt = (((1,), (1,)), ((), ()))
    logits = lax.dot_general(wrh_ref[...], hf_hi, nt, preferred_element_type=F32)
    logits = logits + lax.dot_general(wrh_ref[...], hf_lo, nt, preferred_element_type=F32)
    logits = logits + lax.dot_general(wrl_ref[...], hf_hi, nt, preferred_element_type=F32)
    p = jnp.exp(logits - jnp.max(logits, axis=0, keepdims=True))
    aff_ref[0] = p / jnp.sum(p, axis=0, keepdims=True)


def _mix(o_f, o_b, main, yf4, x, gla_norm_g, w_out, g1, norm2_g, sh2, sc2, w_router_t, tm):
    b, n, d = x.shape
    hv = o_f.shape[-1]
    ne = w_router_t.shape[0]
    n1, ch = yf4.shape[2], yf4.shape[3]
    wr_hi = w_router_t.astype(BF16)
    wr_lo = (w_router_t - wr_hi.astype(F32)).astype(BF16)
    row = lambda w: pl.BlockSpec((1, tm, w), lambda i, j: (i, j, 0))
    vec = pl.BlockSpec((1, 1, d), lambda i, j: (i, 0, 0))
    nblk = d // LANES
    return pl.pallas_call(
        _mix_kernel,
        out_shape=(jax.ShapeDtypeStruct((b, n * nblk, LANES), F32),
                   jax.ShapeDtypeStruct((b, ne, n), F32)),
        grid=(b, n // tm),
        in_specs=[row(hv), row(hv),
                  pl.BlockSpec((1, tm, hv), lambda i, j: (i, j, 2)),
                  pl.BlockSpec((1, tm // n1, n1, ch), lambda i, j: (i, j, 0, 0)),
                  row(d),
                  _resident((1, hv), lambda i, j: (0, 0)),
                  _resident(w_out.shape, lambda i, j: (0, 0)),
                  vec,
                  _resident((1, d), lambda i, j: (0, 0)),
                  vec, vec,
                  _resident((ne, d), lambda i, j: (0, 0)),
                  _resident((ne, d), lambda i, j: (0, 0))],
        out_specs=(pl.BlockSpec((1, tm * nblk, LANES), lambda i, j: (i, j, 0)),
                   pl.BlockSpec((1, ne, tm), lambda i, j: (i, 0, j))),
        compiler_params=_params("parallel", "arbitrary"),
        name="mix",
    )(o_f, o_b, main, yf4, x, gla_norm_g.reshape(1, hv), w_out, g1, norm2_g.reshape(1, d), sh2, sc2,
      wr_hi, wr_lo)


def _per_expert_total(v, nk):
    ne = v.shape[1] // nk
    rows = jnp.concatenate([v[:, e * nk:(e + 1) * nk] for e in range(ne)], axis=0)
    tot = jnp.broadcast_to(jnp.sum(rows, axis=1, keepdims=True), rows.shape)
    return jnp.concatenate([tot[e:e + 1, :] for e in range(ne)], axis=1)


def _token_cumsum(m, nk):
    ne = m.shape[1] // nk
    tril = (lax.broadcasted_iota(I32, (LANES, LANES), 1) <= lax.broadcasted_iota(I32, (LANES, LANES), 0))
    within = jnp.dot(tril.astype(BF16), m.astype(BF16), preferred_element_type=F32)
    coltot = within[LANES - 1:LANES, :]
    rows = jnp.concatenate([coltot[:, e * nk:(e + 1) * nk] for e in range(ne)], axis=0)
    upper = (lax.broadcasted_iota(I32, (nk, nk), 0) < lax.broadcasted_iota(I32, (nk, nk), 1))
    offs = jnp.dot(rows.astype(BF16), upper.astype(BF16), preferred_element_type=F32)
    offs = jnp.concatenate([offs[e:e + 1, :] for e in range(ne)], axis=1)
    return within + offs


def _route_a_kernel(a_ref, cum_ref, pos_ref, start_ref, cnt_ref, *, nk, cap):
    a = a_ref[0]
    ne = a.shape[1] // nk

    def key_value(key):
        return pltpu.bitcast(jnp.where(key < 0, key ^ jnp.int32(0x7FFFFFFF), key), F32)

    def count_ge(key):
        colsum = jnp.sum(jnp.where(a >= key_value(key), 1.0, 0.0), axis=0, keepdims=True)
        return _per_expert_total(colsum, nk)

    fcap = jnp.float32(cap)
    int_min = jnp.full((1, a.shape[1]), -2 ** 31, I32)
    thr = jnp.where(count_ge(jnp.zeros_like(int_min)) >= fcap, 0, int_min)

    def step(i, thr):
        cand = thr + jnp.left_shift(jnp.int32(1), 30 - i)
        return jnp.where(count_ge(cand) >= fcap, cand, thr)

    thr = lax.fori_loop(0, 31, step, thr)
    gt = a >= key_value(thr + 1)
    eq = (a >= key_value(thr)) & jnp.logical_not(gt)
    n_gt = _per_expert_total(jnp.sum(jnp.where(gt, 1.0, 0.0), axis=0, keepdims=True), nk)
    eqf = jnp.where(eq, 1.0, 0.0)
    tie_rank = _token_cumsum(eqf, nk) - eqf
    sel = jnp.where(gt | (eq & (tie_rank < fcap - n_gt)), 1.0, 0.0)
    cum = _token_cumsum(sel, nk)
    cum_ref[0] = cum
    excl = cum - sel
    start = excl[:, 0:nk]
    cnt = sel[:, 0:nk]
    for e in range(1, ne):
        start = start + excl[:, e * nk:(e + 1) * nk]
        cnt = cnt + sel[:, e * nk:(e + 1) * nk]
    start_ref[0] = start
    cnt_ref[0] = cnt
    before = jnp.zeros_like(start)
    for e in range(ne):
        pos_ref[0, :, e * nk:(e + 1) * nk] = start + before
        before = before + sel[:, e * nk:(e + 1) * nk]


def _route_a(a_tab, nk, cap):
    b, _, w = a_tab.shape
    tab = lambda width: pl.BlockSpec((1, LANES, width), lambda i: (i, 0, 0))
    return pl.pallas_call(
        functools.partial(_route_a_kernel, nk=nk, cap=cap),
        out_shape=(jax.ShapeDtypeStruct((b, LANES, w), F32), jax.ShapeDtypeStruct((b, LANES, w), F32),
                   jax.ShapeDtypeStruct((b, LANES, nk), F32), jax.ShapeDtypeStruct((b, LANES, nk), F32)),
        grid=(b,),
        in_specs=[tab(w)],
        out_specs=(tab(w), tab(w), tab(nk), tab(nk)),
        compiler_params=_params("parallel"),
        name="route_a",
    )(a_tab)


def _route_b_kernel(cum_ref, a_ref, pos_ref, tok_ref, gate_ref, dst_ref, *, cap):
    cum = cum_ref[0]
    nk = cum.shape[1]
    col_end = cum[LANES - 1:LANES, :]
    eye = lax.broadcasted_iota(I32, (nk, nk), 0) == lax.broadcasted_iota(I32, (nk, nk), 1)
    col_end_c = jnp.sum(jnp.where(eye, col_end, 0.0), axis=1, keepdims=True)
    s = lax.broadcasted_iota(I32, (1, cap), 1).astype(F32)
    blk = jnp.sum(jnp.where(col_end_c <= s, 1.0, 0.0), axis=0, keepdims=True)
    pick_col = jnp.where(lax.broadcasted_iota(I32, (nk, cap), 0).astype(F32) == blk, 1.0, 0.0)
    gather = lambda tab: jnp.dot(tab, pick_col, preferred_element_type=F32, precision=HIGHEST)
    cum_rows = jnp.round(gather(cum))
    lpos = jnp.sum(jnp.where(cum_rows <= s, 1.0, 0.0), axis=0, keepdims=True)
    pick_row = lax.broadcasted_iota(I32, (LANES, cap), 0).astype(F32) == lpos
    tok_ref[0, 0] = (blk * LANES + lpos).astype(I32)
    gate_ref[0, 0] = jnp.sum(jnp.where(pick_row, gather(a_ref[0]), 0.0), axis=0, keepdims=True)
    dst = jnp.sum(jnp.where(pick_row, gather(pos_ref[0]), 0.0), axis=0, keepdims=True)
    dst_ref[0, 0] = jnp.round(dst).astype(I32)


def _route_b(cum, a_tab, pos, nk, cap):
    b, _, w = cum.shape
    ne = w // nk
    tab = pl.BlockSpec((1, LANES, nk), lambda i, e: (i, 0, e))
    slot = pl.BlockSpec((1, 1, 1, cap), lambda i, e: (i, e, 0, 0))
    sds = lambda dt: jax.ShapeDtypeStruct((b, ne, 1, cap), dt)
    return pl.pallas_call(
        functools.partial(_route_b_kernel, cap=cap),
        out_shape=(sds(I32), sds(F32), sds(I32)),
        grid=(b, ne),
        in_specs=[tab, tab, tab],
        out_specs=(slot, slot, slot),
        compiler_params=_params("parallel", "arbitrary"),
        name="route_b",
    )(cum, a_tab, pos)


def _expert_kernel(tok_ref, tokn_ref, dstp_ref, dst_ref, x1_ref, gate_ref, n2_ref, sh_ref, sc_ref,
                   wg_ref, wu_ref, wd_ref, z_ref, xbuf, ybuf, gsem, ssem):
    nb, nm = pl.num_programs(1), pl.num_programs(2)
    total = pl.num_programs(0) * nb * nm
    b, j = pl.program_id(1), pl.program_id(2)
    s = (pl.program_id(0) * nb + b) * nm + j
    nblk = n2_ref.shape[1] // LANES
    m = xbuf.shape[1] // nblk
    slot = s % 2
    other = 1 - slot
    s_next = jnp.minimum(s + 1, total - 1)
    s_prev = jnp.maximum(s - 1, 0)

    def start_gather(idx_ref, step, buf, rows):
        bb, base = (step // nm) % nb, (step % nm) * m
        for i in rows:
            t = pl.multiple_of(idx_ref[0, 0, base + i] * nblk, nblk)
            pltpu.make_async_copy(x1_ref.at[bb, pl.ds(t, nblk), :], xbuf.at[buf, pl.ds(i * nblk, nblk), :],
                                  gsem.at[buf]).start()

    def start_scatter(idx_ref, step, buf, rows):
        bb, base = (step // nm) % nb, (step % nm) * m
        for i in rows:
            p = pl.multiple_of(idx_ref[0, 0, base + i] * nblk, nblk)
            pltpu.make_async_copy(ybuf.at[buf, pl.ds(i * nblk, nblk), :], z_ref.at[bb, pl.ds(p, nblk), :],
                                  ssem.at[buf]).start()

    def wait_gather(buf):
        pltpu.make_async_copy(x1_ref.at[0, pl.ds(0, m * nblk), :], xbuf.at[buf], gsem.at[buf]).wait()

    def wait_scatter(buf):
        pltpu.make_async_copy(ybuf.at[buf], z_ref.at[0, pl.ds(0, m * nblk), :], ssem.at[buf]).wait()

    @pl.when(s == 0)
    def _():
        start_gather(tok_ref, s, slot, range(m))

    @pl.when(s >= 2)
    def _():
        wait_scatter(slot)

    wait_gather(slot)

    def compute(scatter_prev):
        half = m // 2
        start_gather(tokn_ref, s_next, other, range(0, half))
        x1 = jnp.concatenate(_load_rows(xbuf, (slot,), m, nblk), axis=1)
        x = _rms_mod(x1, n2_ref[...], sh_ref[0], sc_ref[0]).astype(BF16)
        h1 = jnp.dot(x, wg_ref[0], preferred_element_type=F32)
        start_gather(tokn_ref, s_next, other, range(half, m))
        h2 = jnp.dot(x, wu_ref[0], preferred_element_type=F32)
        hid = (h1 * jax.nn.sigmoid(h1) * h2).astype(BF16)
        if scatter_prev:
            start_scatter(dstp_ref, s_prev, other, range(m))
        y = jnp.dot(hid, wd_ref[0], preferred_element_type=F32) * gate_ref[0, 0]
        _store_rows(ybuf, (slot,), y, nblk)

    @pl.when(s == 0)
    def _():
        compute(False)

    @pl.when(s > 0)
    def _():
        compute(True)

    @pl.when(s == total - 1)
    def _():
        start_scatter(dst_ref, s, slot, range(m))
        wait_gather(other)

        @pl.when(s > 0)
        def _():
            wait_scatter(other)

        wait_scatter(slot)


def _expert(tok, dst, x1r, gate_col, norm_g, shift, scale, w_gate, w_up, w_down, m):
    b = x1r.shape[0]
    ne, d, f = w_gate.shape
    nblk = d // LANES
    cap = tok.shape[-1]
    nm = cap // m
    total = ne * b * nm
    vec = pl.BlockSpec((1, 1, d), lambda e, i, j: (i, 0, 0))

    def idx(delta):
        def index_map(e, i, j):
            s = jnp.clip((e * b + i) * nm + j + delta, 0, total - 1)
            return (((s // nm) % b) * ne + s // (b * nm), 0, 0)
        return pl.BlockSpec((1, 1, cap), index_map, memory_space=pltpu.SMEM)

    tok3, dst3 = tok.reshape(b * ne, 1, cap), dst.reshape(b * ne, 1, cap)
    return pl.pallas_call(
        _expert_kernel,
        out_shape=jax.ShapeDtypeStruct((b, ne * cap * nblk, LANES), F32),
        grid=(ne, b, nm),
        in_specs=[idx(0), idx(1), idx(-1), idx(0),
                  pl.BlockSpec(memory_space=pl.ANY),
                  pl.BlockSpec((1, 1, m, 1), lambda e, i, j: (i, e, j, 0)),
                  pl.BlockSpec((1, d), lambda e, i, j: (0, 0)),
                  vec, vec,
                  pl.BlockSpec((1, d, f), lambda e, i, j: (e, 0, 0)),
                  pl.BlockSpec((1, d, f), lambda e, i, j: (e, 0, 0)),
                  pl.BlockSpec((1, f, d), lambda e, i, j: (e, 0, 0))],
        out_specs=pl.BlockSpec(memory_space=pl.ANY),
        scratch_shapes=[pltpu.VMEM((2, m * nblk, LANES), F32), pltpu.VMEM((2, m * nblk, LANES), F32),
                        pltpu.SemaphoreType.DMA((2,)), pltpu.SemaphoreType.DMA((2,))],
        compiler_params=_params("arbitrary", "arbitrary", "arbitrary"),
        name="expert",
    )(tok3, tok3, dst3, dst3, x1r, gate_col, norm_g.reshape(1, d), shift, scale, w_gate, w_up, w_down)


def _combine_kernel(r0_ref, r1_ref, z_ref, st_ref, cn_ref, x1_ref, g2_ref, fg_ref, o_ref,
                    zwin, zbuf, acc, wsem, sem):
    nt = pl.num_programs(1)
    total = pl.num_programs(0) * nt
    idx = pl.program_id(0) * nt + pl.program_id(1)
    slot = idx % 2
    t, d = o_ref.shape[1], o_ref.shape[2]
    nblk = d // LANES
    win, kc = zwin.shape[1] // nblk, zbuf.shape[0] // nblk
    zrows = z_ref.shape[1] // nblk

    def window(k):
        r0a = (r0_ref[k] // SUB) * SUB
        return r0a, pl.multiple_of(jnp.minimum(r0a, zrows - win), SUB)

    def window_copy(k, buf):
        _, rb = window(k)
        return pltpu.make_async_copy(z_ref.at[k // nt, pl.ds(rb * nblk, win * nblk), :], zwin.at[buf],
                                     wsem.at[buf])

    def as_matrix(ref, lead, nrows):
        return jnp.concatenate([v.astype(BF16) for v in _load_rows(ref, lead, nrows, nblk)], axis=1)

    @pl.when(idx == 0)
    def _():
        window_copy(idx, slot).start()

    @pl.when(idx + 1 < total)
    def _():
        window_copy(idx + 1, 1 - slot).start()

    r0a, rb = window(idx)
    r1 = r1_ref[idx]
    start = st_ref[0]
    end = start + cn_ref[0]

    def one_hot(first_row, nominal, width):
        rid = lax.broadcasted_iota(I32, (1, width), 1) + first_row
        ridf = rid.astype(F32)
        mine = (ridf >= start) & (ridf < end) & (rid >= nominal)
        return jnp.where(mine, 1.0, 0.0).astype(BF16)

    p = one_hot(rb, r0a, win)
    window_copy(idx, slot).wait()
    acc[...] = jnp.dot(p, as_matrix(zwin, (slot,), win), preferred_element_type=F32)

    def chunk(c, carry):
        nominal = r0a + win + c * kc
        cb = pl.multiple_of(jnp.minimum(nominal, zrows - kc), SUB)
        cp = pltpu.make_async_copy(z_ref.at[pl.program_id(0), pl.ds(cb * nblk, kc * nblk), :], zbuf, sem)
        cp.start()
        pc = one_hot(cb, nominal, kc)
        cp.wait()
        acc[...] += jnp.dot(pc, as_matrix(zbuf, (), kc), preferred_element_type=F32)
        return carry

    lax.fori_loop(0, jnp.maximum(r1 - (r0a + win) + kc - 1, 0) // kc, chunk, 0)
    x1 = _load_rows(x1_ref, (0,), t, nblk)
    ys = [x1[c] + g2_ref[0][:, c * LANES:(c + 1) * LANES] * acc[:, c * LANES:(c + 1) * LANES]
          for c in range(nblk)]
    ssq = ys[0] * ys[0]
    for c in range(1, nblk):
        ssq = ssq + ys[c] * ys[c]
    rinv = lax.rsqrt(jnp.sum(ssq, axis=-1, keepdims=True) * (1.0 / d) + EPS)
    for c in range(nblk):
        o_ref[0, :, c * LANES:(c + 1) * LANES] = ys[c] * rinv * fg_ref[:, c * LANES:(c + 1) * LANES]


def _combine(r0, r1, z, start_col, cnt_col, x1r, g2, final_g, t, win, kc):
    b = x1r.shape[0]
    d = final_g.shape[0]
    nblk = d // LANES
    n = x1r.shape[1] // nblk
    nt = n // t
    col = pl.BlockSpec((1, t, 1), lambda i, j, *_: (i, j, 0))
    grid_spec = pltpu.PrefetchScalarGridSpec(
        num_scalar_prefetch=2,
        grid=(b, nt),
        in_specs=[pl.BlockSpec(memory_space=pl.ANY), col, col,
                  pl.BlockSpec((1, t * nblk, LANES), lambda i, j, *_: (i, j, 0)),
                  pl.BlockSpec((1, 1, d), lambda i, j, *_: (i, 0, 0)),
                  pl.BlockSpec((1, d), lambda i, j, *_: (0, 0))],
        out_specs=pl.BlockSpec((1, t, d), lambda i, j, *_: (i, j, 0)),
        scratch_shapes=[pltpu.VMEM((2, win * nblk, LANES), F32), pltpu.VMEM((kc * nblk, LANES), F32),
                        pltpu.VMEM((t, d), F32), pltpu.SemaphoreType.DMA((2,)), pltpu.SemaphoreType.DMA],
    )
    return pl.pallas_call(
        _combine_kernel,
        out_shape=jax.ShapeDtypeStruct((b, n, d), F32),
        grid_spec=grid_spec,
        compiler_params=_params("arbitrary", "arbitrary"),
        name="combine",
    )(r0, r1, z, start_col, cnt_col, x1r, g2, final_g.reshape(1, d))


def kernel(x, c, ctx, c_ctx, w_ada, b_ada, norm1_g, w_in, w_a2_f, b_a2_f, w_a2_b, b_a2_b,
           gla_norm_g, w_out, norm2_g, w_router, w_e_gate, w_e_up, w_e_down, final_norm_g):
    b, n, d = x.shape
    depth = w_ada.shape[0]
    assert depth == 1, "the context stream update of non-final layers is not implemented"
    assert n % (LANES * SUB) == 0 and d % LANES == 0
    hk, hv = GLA_HEADS * GLA_DK, GLA_HEADS * GLA_DV
    nk = n // LANES
    cap = CAPACITY_FACTOR * n // N_EXPERTS
    lyr = 0

    cond = jnp.zeros((SUB, d), F32).at[:b].set(c).at[b].set(c_ctx)
    ada = _ada(cond, w_ada[lyr], b_ada[lyr])
    sh1, sc1, g1, sh2, sc2, g2 = [ada[:b, j * d:(j + 1) * d].reshape(b, 1, d) for j in range(6)]
    sh1_c, sc1_c = [jnp.broadcast_to(ada[b, j * d:(j + 1) * d], (b, 1, d)) for j in range(2)]

    w = w_in[lyr]
    s_main = 2 * hk + 2 * hv
    w_main = w[:, :s_main].astype(BF16)
    w_z = jnp.pad(w[:, s_main:s_main + 2 * GATE_RANK], ((0, 0), (0, LANES - 2 * GATE_RANK))).astype(BF16)
    w_u = w[:, s_main + 2 * GATE_RANK:].astype(BF16)
    main_c, _, z_c = _inproj(ctx, norm1_g[lyr], sh1_c, sc1_c, w_main, w_u, w_z, tm=ctx.shape[1])
    main_l, u_l, z_l = _inproj(x, norm1_g[lyr], sh1, sc1, w_main, w_u, w_z, tm=512)

    zero = jnp.zeros((b, GLA_HEADS, GLA_DK, GLA_DV), F32)
    _, s_f = _gla(main_c, z_c, w_a2_f[lyr], b_a2_f[lyr], zero, reverse=False, r=ctx.shape[1])
    _, s_b = _gla(main_c, z_c, w_a2_b[lyr], b_a2_b[lyr], zero, reverse=True, r=ctx.shape[1])
    o_f, _ = _gla(main_l, z_l, w_a2_f[lyr], b_a2_f[lyr], s_f, reverse=False, r=256)
    o_b, _ = _gla(main_l, z_l, w_a2_b[lyr], b_a2_b[lyr], s_b, reverse=True, r=256)

    w1, w2, csm = _dft_tables(n)
    yf4 = _four_b(_four_a(u_l, w1), w2, csm)

    x1r, aff_t = _mix(o_f, o_b, main_l, yf4, x, gla_norm_g[lyr], w_out[lyr].astype(BF16), g1,
                      norm2_g[lyr], sh2, sc2, w_router[lyr].T, tm=512)

    a_tab = aff_t.reshape(b, N_EXPERTS, nk, LANES).transpose(0, 3, 1, 2).reshape(b, LANES, N_EXPERTS * nk)
    cum, pos, start, cnt = _route_a(a_tab, nk, cap)
    tok, gate, dst = _route_b(cum, a_tab, pos, nk, cap)

    z = _expert(tok, dst, x1r, gate.reshape(b, N_EXPERTS, cap, 1), norm2_g[lyr], sh2, sc2,
                w_e_gate[lyr].astype(BF16), w_e_up[lyr].astype(BF16), w_e_down[lyr].astype(BF16), m=256)

    t_tile = 512
    start_tok = start.transpose(0, 2, 1).reshape(b, n)
    cnt_tok = cnt.transpose(0, 2, 1).reshape(b, n)
    r0 = start_tok[:, ::t_tile].astype(I32)
    r1 = jnp.concatenate([r0[:, 1:], jnp.full((b, 1), N_EXPERTS * cap, I32)], axis=1)
    return _combine(r0.reshape(-1), r1.reshape(-1), z, start_tok.reshape(b, n, 1), cnt_tok.reshape(b, n, 1),
                    x1r, g2, final_norm_g, t=t_tile, win=5 * t_tile // 2, kc=256)
```

```python
import functools

import jax
import jax.numpy as jnp
import numpy as np
from jax import lax
from jax.experimental import pallas as pl
from jax.experimental.pallas import tpu as pltpu

F32 = jnp.float32
BF16 = jnp.bfloat16
I32 = jnp.int32
HIGHEST = lax.Precision.HIGHEST

EPS = 1e-6
GLA_HEADS = 4
GLA_DK = 128
GLA_DV = 256
GATE_RANK = 16
GATE_NORMALIZER = 16.0
FOURIER_GROUP_DIM = 128
N_EXPERTS = 16
CAPACITY_FACTOR = 2
LANES = 128
VMEM_LIMIT = 56 * 1024 * 1024


def _params(*sem):
    return pltpu.CompilerParams(dimension_semantics=sem, vmem_limit_bytes=VMEM_LIMIT)


def _resident(block_shape, index_map):
    return pl.BlockSpec(block_shape, index_map, pipeline_mode=pl.Buffered(1))


def _ada_kernel(c_ref, w_ref, b_ref, o_ref):
    c = c_ref[...]
    s = c * jax.nn.sigmoid(c)
    o_ref[...] = jnp.dot(s.astype(BF16), w_ref[...].astype(BF16),
                         preferred_element_type=F32) + b_ref[...]


def _ada(cond, w_ada, b_ada):
    rows, d = cond.shape
    n = w_ada.shape[1]
    tn = 1024
    return pl.pallas_call(
        _ada_kernel,
        out_shape=jax.ShapeDtypeStruct((rows, n), F32),
        grid=(n // tn,),
        in_specs=[pl.BlockSpec((rows, d), lambda j: (0, 0)),
                  pl.BlockSpec((d, tn), lambda j: (0, j)),
                  pl.BlockSpec((1, tn), lambda j: (0, j))],
        out_specs=pl.BlockSpec((rows, tn), lambda j: (0, j)),
        compiler_params=_params("arbitrary"),
        name="ada",
    )(cond, w_ada, b_ada.reshape(1, n))


def _rms_mod(x, g, shift, scale):
    y = x * lax.rsqrt(jnp.mean(x * x, axis=-1, keepdims=True) + EPS)
    return (y * g) * (1.0 + scale) + shift


def _inproj_kernel(x_ref, g_ref, sh_ref, sc_ref, wm_ref, wu_ref, wz_ref, om_ref, ou_ref, oz_ref):
    h = _rms_mod(x_ref[0], g_ref[...], sh_ref[0], sc_ref[0]).astype(BF16)
    n = wm_ref.shape[1]
    step = 1024
    for j in range(n // step):
        om_ref[0, :, j * step:(j + 1) * step] = jnp.dot(
            h, wm_ref[:, j * step:(j + 1) * step], preferred_element_type=F32).astype(BF16)
    ou_ref[0] = jnp.dot(h, wu_ref[...], preferred_element_type=F32).reshape(ou_ref.shape[1:])
    oz_ref[0] = jnp.dot(h, wz_ref[...], preferred_element_type=F32)


def _inproj(x, norm_g, shift, scale, w_main, w_u, w_z, tm):
    b, n, d = x.shape
    nm = w_main.shape[1]
    nu = w_u.shape[1]
    return pl.pallas_call(
        _inproj_kernel,
        out_shape=(jax.ShapeDtypeStruct((b, n, nm), BF16),
                   jax.ShapeDtypeStruct((b, n // LANES, LANES, nu), F32),
                   jax.ShapeDtypeStruct((b, n, LANES), F32)),
        grid=(b, n // tm),
        in_specs=[pl.BlockSpec((1, tm, d), lambda i, j: (i, j, 0)),
                  _resident((1, d), lambda i, j: (0, 0)),
                  pl.BlockSpec((1, 1, d), lambda i, j: (i, 0, 0)),
                  pl.BlockSpec((1, 1, d), lambda i, j: (i, 0, 0)),
                  _resident((d, nm), lambda i, j: (0, 0)),
                  _resident((d, nu), lambda i, j: (0, 0)),
                  _resident((d, LANES), lambda i, j: (0, 0))],
        out_specs=(pl.BlockSpec((1, tm, nm), lambda i, j: (i, j, 0)),
                   pl.BlockSpec((1, tm // LANES, LANES, nu), lambda i, j: (i, j, 0, 0)),
                   pl.BlockSpec((1, tm, LANES), lambda i, j: (i, j, 0))),
        compiler_params=_params("parallel", "arbitrary"),
        name="inproj",
    )(x, norm_g.reshape(1, d), shift, scale, w_main, w_u, w_z)


def _split3(x):
    p1 = x.astype(BF16)
    r1 = x - p1.astype(F32)
    p2 = r1.astype(BF16)
    p3 = (r1 - p2.astype(F32)).astype(BF16)
    return p1, p2, p3


def _gla_block(q_ref, k_ref, v_ref, z_ref, wah_ref, wal_ref, ba_ref, o_ref, st_ref, reverse):
    r = q_ref.shape[1]
    z1, z2, _ = _split3(z_ref[0])
    pre = jnp.dot(z1, wah_ref[...], preferred_element_type=F32)
    pre = pre + jnp.dot(z2, wah_ref[...], preferred_element_type=F32)
    pre = pre + jnp.dot(z1, wal_ref[...], preferred_element_type=F32) + ba_ref[...]
    la = jax.nn.log_sigmoid(pre) * (1.0 / GATE_NORMALIZER)
    row = lax.broadcasted_iota(I32, (r, r), 0)
    col = lax.broadcasted_iota(I32, (r, r), 1)
    keep = (col >= row) if reverse else (col <= row)
    tri = jnp.where(keep, 1.0, 0.0).astype(BF16)
    c = sum(jnp.dot(tri, piece, preferred_element_type=F32) for piece in _split3(la))
    tot = c[0:1, :] if reverse else c[r - 1:r, :]
    ref = 0.5 * tot
    e_in = jnp.exp(c)
    e_q = jnp.exp(c - ref)
    e_k = jnp.exp(ref - c)
    e_end = jnp.exp(tot - c)
    e_tot = jnp.exp(tot)
    q = q_ref[0].astype(F32) * (GLA_DK ** -0.5)
    k = k_ref[0].astype(F32)
    q_in = (q * e_in).astype(BF16)
    q_d = (q * e_q).astype(BF16)
    k_i = (k * e_k).astype(BF16)
    k_e = (k * e_end).astype(BF16)
    v = v_ref[0]
    eye = lax.broadcasted_iota(I32, (GLA_DK, GLA_DK), 0) == lax.broadcasted_iota(I32, (GLA_DK, GLA_DK), 1)
    for h in range(GLA_HEADS):
        ks = slice(h * GLA_DK, (h + 1) * GLA_DK)
        vs = slice(h * GLA_DV, (h + 1) * GLA_DV)
        s_in = st_ref[h]
        sc = lax.dot_general(q_d[:, ks], k_i[:, ks], (((1,), (1,)), ((), ())),
                             preferred_element_type=F32)
        sc = jnp.where(keep, sc, 0.0).astype(BF16)
        o = jnp.dot(sc, v[:, vs], preferred_element_type=F32)
        o = o + jnp.dot(q_in[:, ks], s_in.astype(BF16), preferred_element_type=F32)
        o_ref[0, :, vs] = o
        kv = lax.dot_general(k_e[:, ks], v[:, vs], (((0,), (0,)), ((), ())),
                             preferred_element_type=F32)
        dcol = jnp.sum(jnp.where(eye, e_tot[:, ks], 0.0), axis=1, keepdims=True)
        st_ref[h] = dcol * s_in + kv


def _gla_kernel(qf, kf, vf, zf, qb, kb, vb, zb, wfh, wfl, bf, wbh, wbl, bb, s0f, s0b,
                of_ref, ob_ref, sff, sfb, stf, stb):
    j = pl.program_id(1)

    @pl.when(j == 0)
    def _():
        stf[...] = s0f[0]
        stb[...] = s0b[0]

    _gla_block(qf, kf, vf, zf, wfh, wfl, bf, of_ref, stf, False)
    _gla_block(qb, kb, vb, zb, wbh, wbl, bb, ob_ref, stb, True)

    @pl.when(j == pl.num_programs(1) - 1)
    def _():
        sff[0] = stf[...]
        sfb[0] = stb[...]


def _gla(main, z, gates_f, gates_b, s0_f, s0_b, r):
    b, n, _ = main.shape
    nb = n // r
    hk = GLA_HEADS * GLA_DK
    hv = GLA_HEADS * GLA_DV
    fwd = lambda j: j
    bwd = lambda j: nb - 1 - j

    def streams(pos):
        return [pl.BlockSpec((1, r, hk), lambda i, j: (i, pos(j), 0)),
                pl.BlockSpec((1, r, hk), lambda i, j: (i, pos(j), 1)),
                pl.BlockSpec((1, r, hv), lambda i, j: (i, pos(j), 1)),
                pl.BlockSpec((1, r, LANES), lambda i, j: (i, pos(j), 0))]

    gate_specs = [_resident((LANES, hk), lambda i, j: (0, 0)), _resident((LANES, hk), lambda i, j: (0, 0)),
                  _resident((1, hk), lambda i, j: (0, 0))]
    state = pl.BlockSpec((1, GLA_HEADS, GLA_DK, GLA_DV), lambda i, j: (i, 0, 0, 0))
    state_sds = jax.ShapeDtypeStruct((b, GLA_HEADS, GLA_DK, GLA_DV), F32)
    return pl.pallas_call(
        _gla_kernel,
        out_shape=(jax.ShapeDtypeStruct((b, n, hv), F32), jax.ShapeDtypeStruct((b, n, hv), F32),
                   state_sds, state_sds),
        grid=(b, nb),
        in_specs=streams(fwd) + streams(bwd) + gate_specs + gate_specs + [state, state],
        out_specs=(pl.BlockSpec((1, r, hv), lambda i, j: (i, fwd(j), 0)),
                   pl.BlockSpec((1, r, hv), lambda i, j: (i, bwd(j), 0)), state, state),
        scratch_shapes=[pltpu.VMEM((GLA_HEADS, GLA_DK, GLA_DV), F32),
                        pltpu.VMEM((GLA_HEADS, GLA_DK, GLA_DV), F32)],
        compiler_params=_params("parallel", "arbitrary"),
        name="gla",
    )(main, main, main, z, main, main, main, z, *gates_f, *gates_b, s0_f, s0_b)


def _gate_weights(w, bias, offset):
    wp = jnp.zeros((LANES, w.shape[1]), F32).at[offset:offset + w.shape[0]].set(w)
    hi = wp.astype(BF16)
    lo = (wp - hi.astype(F32)).astype(BF16)
    return hi, lo, bias.reshape(1, -1)


SUB = 8


def _four_a_kernel(x_ref, w_ref, o_ref):
    _, n1, sub, ch = x_ref.shape
    x = x_ref[0].reshape(n1 * sub, ch).astype(BF16)
    a = jnp.dot(w_ref[...], x, preferred_element_type=F32)
    o_ref[0] = a.reshape(2, n1, sub, ch)


def _four_a(u4, w1s):
    b, n1, n2, ch = u4.shape
    return pl.pallas_call(
        _four_a_kernel,
        out_shape=jax.ShapeDtypeStruct((b, 2, n1, n2, ch), F32),
        grid=(b, n2 // SUB),
        in_specs=[pl.BlockSpec((1, n1, SUB, ch), lambda i, j: (i, 0, j, 0)),
                  _resident(w1s.shape, lambda i, j: (0, 0))],
        out_specs=pl.BlockSpec((1, 2, n1, SUB, ch), lambda i, j: (i, 0, 0, j, 0)),
        compiler_params=_params("parallel", "arbitrary"),
        name="four_a",
    )(u4, w1s)


def _four_b_kernel(a_ref, w_ref, cs_ref, o_ref):
    ch = o_ref.shape[-1]
    ng = ch // FOURIER_GROUP_DIM
    cs = cs_ref[...]
    for j in range(SUB):
        a = jnp.concatenate([a_ref[0, 0, j], a_ref[0, 1, j]], axis=0).astype(BF16)
        a2 = jnp.dot(w_ref[j], a, preferred_element_type=F32).astype(BF16)
        n2 = a2.shape[0] // 2
        lhs = jnp.concatenate(
            [jnp.concatenate([a2[:n2, g * LANES:(g + 1) * LANES], a2[n2:, g * LANES:(g + 1) * LANES]], axis=1)
             for g in range(ng)], axis=0)
        y = jnp.dot(lhs, cs, preferred_element_type=F32)
        y = jnp.concatenate([y[g * n2:(g + 1) * n2] for g in range(ng)], axis=1)
        o_ref[0, :, j, :] = y


def _four_b(a, w2, cs):
    b, _, n1, n2, ch = a.shape
    return pl.pallas_call(
        _four_b_kernel,
        out_shape=jax.ShapeDtypeStruct((b, n2, n1, ch), F32),
        grid=(b, n1 // SUB),
        in_specs=[pl.BlockSpec((1, 2, SUB, n2, ch), lambda i, j: (i, 0, j, 0, 0)),
                  pl.BlockSpec((SUB, 2 * n2, 2 * n2), lambda i, j: (j, 0, 0)),
                  _resident((2 * FOURIER_GROUP_DIM, FOURIER_GROUP_DIM), lambda i, j: (0, 0))],
        out_specs=pl.BlockSpec((1, n2, SUB, ch), lambda i, j: (i, 0, j, 0)),
        compiler_params=_params("parallel", "arbitrary"),
        name="four_b",
    )(a, w2, cs)


def _dft_tables(n):
    n1 = n // LANES

    def cs(num, den, count):
        ang = (num % den).astype(F32) * F32(2.0 * np.pi / den)
        scale = float(count) ** -0.5
        return jnp.cos(ang) * scale, jnp.sin(ang) * scale

    i1 = jnp.arange(n1, dtype=I32)
    il = jnp.arange(LANES, dtype=I32)
    ri = jnp.arange(2 * n1 * SUB, dtype=I32)[:, None]
    ci = jnp.arange(n1 * SUB, dtype=I32)[None, :]
    c1, s1 = cs(((ri // SUB) % n1) * (ci // SUB), n1, n1)
    w1 = jnp.where(ri % SUB == ci % SUB, jnp.where(ri < n1 * SUB, c1, -s1), 0.0).astype(BF16)
    kk = i1[:, None, None] + n1 * il[None, :, None]
    c2, s2 = cs(kk * il[None, None, :], n, LANES)
    w2 = jnp.concatenate([jnp.concatenate([c2, s2], axis=2),
                          jnp.concatenate([-s2, c2], axis=2)], axis=1).astype(BF16)
    cc, sc = cs(il[:, None] * il[None, :], FOURIER_GROUP_DIM, FOURIER_GROUP_DIM)
    csm = jnp.concatenate([cc, sc], axis=0).astype(BF16)
    return w1, w2, csm


def _mix_kernel(of_ref, ob_ref, g_ref, yf_ref, x_ref, gn_ref, wo_ref, g1_ref, n2_ref, sh_ref, sc_ref,
                wrh_ref, wrl_ref, x1_ref, aff_ref):
    tm, d = x_ref.shape[1], x_ref.shape[2]
    o = of_ref[0] + ob_ref[0]
    heads = []
    for h in range(GLA_HEADS):
        oh = o[:, h * GLA_DV:(h + 1) * GLA_DV]
        heads.append(oh * lax.rsqrt(jnp.mean(oh * oh, axis=-1, keepdims=True) + EPS))
    on = jnp.concatenate(heads, axis=1) * gn_ref[...]
    g = g_ref[0].astype(F32)
    y_gla = (on * (g * jax.nn.sigmoid(g))).astype(BF16)
    hw = y_gla.shape[1]
    yf = yf_ref[0].reshape(tm, yf_ref.shape[-1]).astype(BF16)
    acc = jnp.dot(y_gla, wo_ref[:hw, :], preferred_element_type=F32)
    acc = acc + jnp.dot(yf, wo_ref[hw:, :], preferred_element_type=F32)
    x1 = x_ref[0] + g1_ref[0] * acc
    x1_ref[0] = x1
    hf = _rms_mod(x1, n2_ref[...], sh_ref[0], sc_ref[0])
    hf_hi = hf.astype(BF16)
    hf_lo = (hf - hf_hi.astype(F32)).astype(BF16)
    nt = (((1,), (1,)), ((), ()))
    logits = lax.dot_general(wrh_ref[...], hf_hi, nt, preferred_element_type=F32)
    logits = logits + lax.dot_general(wrh_ref[...], hf_lo, nt, preferred_element_type=F32)
    logits = logits + lax.dot_general(wrl_ref[...], hf_hi, nt, preferred_element_type=F32)
    p = jnp.exp(logits - jnp.max(logits, axis=0, keepdims=True))
    aff_ref[0] = p / jnp.sum(p, axis=0, keepdims=True)


def _mix(o_f, o_b, main, yf4, x, gla_norm_g, w_out, g1, norm2_g, sh2, sc2, w_router_t, tm):
    b, n, d = x.shape
    hv = o_f.shape[-1]
    ne = w_router_t.shape[0]
    n1, ch = yf4.shape[2], yf4.shape[3]
    wr_hi = w_router_t.astype(BF16)
    wr_lo = (w_router_t - wr_hi.astype(F32)).astype(BF16)
    row = lambda w: pl.BlockSpec((1, tm, w), lambda i, j: (i, j, 0))
    vec = pl.BlockSpec((1, 1, d), lambda i, j: (i, 0, 0))
    return pl.pallas_call(
        _mix_kernel,
        out_shape=(jax.ShapeDtypeStruct((b, n, d), F32),
                   jax.ShapeDtypeStruct((b, ne, n), F32)),
        grid=(b, n // tm),
        in_specs=[row(hv), row(hv),
                  pl.BlockSpec((1, tm, hv), lambda i, j: (i, j, 2)),
                  pl.BlockSpec((1, tm // n1, n1, ch), lambda i, j: (i, j, 0, 0)),
                  row(d),
                  _resident((1, hv), lambda i, j: (0, 0)),
                  _resident(w_out.shape, lambda i, j: (0, 0)),
                  vec,
                  _resident((1, d), lambda i, j: (0, 0)),
                  vec, vec,
                  _resident((ne, d), lambda i, j: (0, 0)),
                  _resident((ne, d), lambda i, j: (0, 0))],
        out_specs=(row(d), pl.BlockSpec((1, ne, tm), lambda i, j: (i, 0, j))),
        compiler_params=_params("parallel", "arbitrary"),
        name="mix",
    )(o_f, o_b, main, yf4, x, gla_norm_g.reshape(1, hv), w_out, g1, norm2_g.reshape(1, d), sh2, sc2,
      wr_hi, wr_lo)


def _per_expert_total(v, nk):
    ne = v.shape[1] // nk
    rows = jnp.concatenate([v[:, e * nk:(e + 1) * nk] for e in range(ne)], axis=0)
    tot = jnp.broadcast_to(jnp.sum(rows, axis=1, keepdims=True), rows.shape)
    return jnp.concatenate([tot[e:e + 1, :] for e in range(ne)], axis=1)


def _token_cumsum(m, nk):
    ne = m.shape[1] // nk
    tril = (lax.broadcasted_iota(I32, (LANES, LANES), 1) <= lax.broadcasted_iota(I32, (LANES, LANES), 0))
    within = jnp.dot(tril.astype(BF16), m.astype(BF16), preferred_element_type=F32)
    coltot = within[LANES - 1:LANES, :]
    rows = jnp.concatenate([coltot[:, e * nk:(e + 1) * nk] for e in range(ne)], axis=0)
    upper = (lax.broadcasted_iota(I32, (nk, nk), 0) < lax.broadcasted_iota(I32, (nk, nk), 1))
    offs = jnp.dot(rows.astype(BF16), upper.astype(BF16), preferred_element_type=F32)
    offs = jnp.concatenate([offs[e:e + 1, :] for e in range(ne)], axis=1)
    return within + offs


def _route_a_kernel(a_ref, cum_ref, pos_ref, start_ref, cnt_ref, *, nk, cap):
    a = a_ref[0]
    ne = a.shape[1] // nk

    def key_value(key):
        return pltpu.bitcast(jnp.where(key < 0, key ^ jnp.int32(0x7FFFFFFF), key), F32)

    def count_ge(key):
        colsum = jnp.sum(jnp.where(a >= key_value(key), 1.0, 0.0), axis=0, keepdims=True)
        return _per_expert_total(colsum, nk)

    fcap = jnp.float32(cap)
    int_min = jnp.full((1, a.shape[1]), -2 ** 31, I32)
    thr = jnp.where(count_ge(jnp.zeros_like(int_min)) >= fcap, 0, int_min)

    def step(i, thr):
        cand = thr + jnp.left_shift(jnp.int32(1), 30 - i)
        return jnp.where(count_ge(cand) >= fcap, cand, thr)

    thr = lax.fori_loop(0, 31, step, thr)
    gt = a >= key_value(thr + 1)
    eq = (a >= key_value(thr)) & jnp.logical_not(gt)
    n_gt = _per_expert_total(jnp.sum(jnp.where(gt, 1.0, 0.0), axis=0, keepdims=True), nk)
    eqf = jnp.where(eq, 1.0, 0.0)
    tie_rank = _token_cumsum(eqf, nk) - eqf
    sel = jnp.where(gt | (eq & (tie_rank < fcap - n_gt)), 1.0, 0.0)
    cum = _token_cumsum(sel, nk)
    cum_ref[0] = cum
    excl = cum - sel
    start = excl[:, 0:nk]
    cnt = sel[:, 0:nk]
    for e in range(1, ne):
        start = start + excl[:, e * nk:(e + 1) * nk]
        cnt = cnt + sel[:, e * nk:(e + 1) * nk]
    start_ref[0] = start
    cnt_ref[0] = cnt
    before = jnp.zeros_like(start)
    for e in range(ne):
        pos_ref[0, :, e * nk:(e + 1) * nk] = start + before
        before = before + sel[:, e * nk:(e + 1) * nk]


def _route_a(a_tab, nk, cap):
    b, _, w = a_tab.shape
    tab = lambda width: pl.BlockSpec((1, LANES, width), lambda i: (i, 0, 0))
    return pl.pallas_call(
        functools.partial(_route_a_kernel, nk=nk, cap=cap),
        out_shape=(jax.ShapeDtypeStruct((b, LANES, w), F32), jax.ShapeDtypeStruct((b, LANES, w), F32),
                   jax.ShapeDtypeStruct((b, LANES, nk), F32), jax.ShapeDtypeStruct((b, LANES, nk), F32)),
        grid=(b,),
        in_specs=[tab(w)],
        out_specs=(tab(w), tab(w), tab(nk), tab(nk)),
        compiler_params=_params("parallel"),
        name="route_a",
    )(a_tab)


def _route_b_kernel(cum_ref, a_ref, pos_ref, tok_ref, gate_ref, dst_ref, *, cap):
    cum = cum_ref[0]
    nk = cum.shape[1]
    col_end = cum[LANES - 1:LANES, :]
    eye = lax.broadcasted_iota(I32, (nk, nk), 0) == lax.broadcasted_iota(I32, (nk, nk), 1)
    col_end_c = jnp.sum(jnp.where(eye, col_end, 0.0), axis=1, keepdims=True)
    s = lax.broadcasted_iota(I32, (1, cap), 1).astype(F32)
    blk = jnp.sum(jnp.where(col_end_c <= s, 1.0, 0.0), axis=0, keepdims=True)
    pick_col = jnp.where(lax.broadcasted_iota(I32, (nk, cap), 0).astype(F32) == blk, 1.0, 0.0)
    gather = lambda tab: jnp.dot(tab, pick_col, preferred_element_type=F32, precision=HIGHEST)
    cum_rows = jnp.round(gather(cum))
    lpos = jnp.sum(jnp.where(cum_rows <= s, 1.0, 0.0), axis=0, keepdims=True)
    pick_row = lax.broadcasted_iota(I32, (LANES, cap), 0).astype(F32) == lpos
    tok_ref[0, 0] = (blk * LANES + lpos).astype(I32)
    gate_ref[0, 0] = jnp.sum(jnp.where(pick_row, gather(a_ref[0]), 0.0), axis=0, keepdims=True)
    dst = jnp.sum(jnp.where(pick_row, gather(pos_ref[0]), 0.0), axis=0, keepdims=True)
    dst_ref[0, 0] = jnp.round(dst).astype(I32)


def _route_b(cum, a_tab, pos, nk, cap):
    b, _, w = cum.shape
    ne = w // nk
    tab = pl.BlockSpec((1, LANES, nk), lambda i, e: (i, 0, e))
    slot = pl.BlockSpec((1, 1, 1, cap), lambda i, e: (i, e, 0, 0))
    sds = lambda dt: jax.ShapeDtypeStruct((b, ne, 1, cap), dt)
    return pl.pallas_call(
        functools.partial(_route_b_kernel, cap=cap),
        out_shape=(sds(I32), sds(F32), sds(I32)),
        grid=(b, ne),
        in_specs=[tab, tab, tab],
        out_specs=(slot, slot, slot),
        compiler_params=_params("parallel", "arbitrary"),
        name="route_b",
    )(cum, a_tab, pos)


def _expert_kernel(tok_ref, tokn_ref, dstp_ref, dst_ref, x1_ref, gate_ref, n2_ref, sh_ref, sc_ref,
                   wg_ref, wu_ref, wd_ref, z_ref, xbuf, ybuf, gsem, ssem):
    nb, nm = pl.num_programs(1), pl.num_programs(2)
    total = pl.num_programs(0) * nb * nm
    b, j = pl.program_id(1), pl.program_id(2)
    s = (pl.program_id(0) * nb + b) * nm + j
    m = xbuf.shape[1]
    slot = s % 2
    other = 1 - slot
    s_next = jnp.minimum(s + 1, total - 1)
    s_prev = jnp.maximum(s - 1, 0)

    def start_gather(idx_ref, step, buf, rows):
        bb, base = (step // nm) % nb, (step % nm) * m
        for i in rows:
            t = idx_ref[0, 0, base + i]
            pltpu.make_async_copy(x1_ref.at[bb, pl.ds(t, 1), :], xbuf.at[buf, pl.ds(i, 1), :],
                                  gsem.at[buf]).start()

    def start_scatter(idx_ref, step, buf, rows):
        bb, base = (step // nm) % nb, (step % nm) * m
        for i in rows:
            p = idx_ref[0, 0, base + i]
            pltpu.make_async_copy(ybuf.at[buf, pl.ds(i, 1), :], z_ref.at[bb, pl.ds(p, 1), :],
                                  ssem.at[buf]).start()

    def wait_gather(buf):
        pltpu.make_async_copy(x1_ref.at[0, pl.ds(0, m), :], xbuf.at[buf], gsem.at[buf]).wait()

    def wait_scatter(buf):
        pltpu.make_async_copy(ybuf.at[buf], z_ref.at[0, pl.ds(0, m), :], ssem.at[buf]).wait()

    @pl.when(s == 0)
    def _():
        start_gather(tok_ref, s, slot, range(m))

    @pl.when(s >= 2)
    def _():
        wait_scatter(slot)

    wait_gather(slot)

    def compute(scatter_prev):
        half = m // 2
        start_gather(tokn_ref, s_next, other, range(0, half))
        x = _rms_mod(xbuf[slot], n2_ref[...], sh_ref[0], sc_ref[0]).astype(BF16)
        h1 = jnp.dot(x, wg_ref[0], preferred_element_type=F32)
        start_gather(tokn_ref, s_next, other, range(half, m))
        h2 = jnp.dot(x, wu_ref[0], preferred_element_type=F32)
        hid = (h1 * jax.nn.sigmoid(h1) * h2).astype(BF16)
        if scatter_prev:
            start_scatter(dstp_ref, s_prev, other, range(m))
        ybuf[slot] = jnp.dot(hid, wd_ref[0], preferred_element_type=F32) * gate_ref[0, 0]

    @pl.when(s == 0)
    def _():
        compute(False)

    @pl.when(s > 0)
    def _():
        compute(True)

    @pl.when(s == total - 1)
    def _():
        start_scatter(dst_ref, s, slot, range(m))
        wait_gather(other)

        @pl.when(s > 0)
        def _():
            wait_scatter(other)

        wait_scatter(slot)


def _expert(tok, dst, x1, gate_col, norm_g, shift, scale, w_gate, w_up, w_down, m):
    b = x1.shape[0]
    ne, d, f = w_gate.shape
    cap = tok.shape[-1]
    nm = cap // m
    total = ne * b * nm
    vec = pl.BlockSpec((1, 1, d), lambda e, i, j: (i, 0, 0))

    def idx(delta):
        def index_map(e, i, j):
            s = jnp.clip((e * b + i) * nm + j + delta, 0, total - 1)
            return (((s // nm) % b) * ne + s // (b * nm), 0, 0)
        return pl.BlockSpec((1, 1, cap), index_map, memory_space=pltpu.SMEM)

    tok3, dst3 = tok.reshape(b * ne, 1, cap), dst.reshape(b * ne, 1, cap)
    return pl.pallas_call(
        _expert_kernel,
        out_shape=jax.ShapeDtypeStruct((b, ne * cap, d), F32),
        grid=(ne, b, nm),
        in_specs=[idx(0), idx(1), idx(-1), idx(0),
                  pl.BlockSpec(memory_space=pl.ANY),
                  pl.BlockSpec((1, 1, m, 1), lambda e, i, j: (i, e, j, 0)),
                  pl.BlockSpec((1, d), lambda e, i, j: (0, 0)),
                  vec, vec,
                  pl.BlockSpec((1, d, f), lambda e, i, j: (e, 0, 0)),
                  pl.BlockSpec((1, d, f), lambda e, i, j: (e, 0, 0)),
                  pl.BlockSpec((1, f, d), lambda e, i, j: (e, 0, 0))],
        out_specs=pl.BlockSpec(memory_space=pl.ANY),
        scratch_shapes=[pltpu.VMEM((2, m, d), F32), pltpu.VMEM((2, m, d), F32),
                        pltpu.SemaphoreType.DMA((2,)), pltpu.SemaphoreType.DMA((2,))],
        compiler_params=_params("arbitrary", "arbitrary", "arbitrary"),
        name="expert",
    )(tok3, tok3, dst3, dst3, x1, gate_col, norm_g.reshape(1, d), shift, scale, w_gate, w_up, w_down)


def _combine_kernel(r0_ref, r1_ref, z_ref, st_ref, cn_ref, x1_ref, g2_ref, fg_ref, o_ref,
                    zwin, zbuf, acc, wsem, sem):
    nt = pl.num_programs(1)
    total = pl.num_programs(0) * nt
    idx = pl.program_id(0) * nt + pl.program_id(1)
    slot = idx % 2
    win, kc = zwin.shape[1], zbuf.shape[0]
    zrows = z_ref.shape[1]

    def window(k):
        r0a = (r0_ref[k] // SUB) * SUB
        return r0a, pl.multiple_of(jnp.minimum(r0a, zrows - win), SUB)

    def window_copy(k, buf):
        _, rb = window(k)
        return pltpu.make_async_copy(z_ref.at[k // nt, pl.ds(rb, win), :], zwin.at[buf], wsem.at[buf])

    @pl.when(idx == 0)
    def _():
        window_copy(idx, slot).start()

    @pl.when(idx + 1 < total)
    def _():
        window_copy(idx + 1, 1 - slot).start()

    r0a, rb = window(idx)
    r1 = r1_ref[idx]
    start = st_ref[0]
    end = start + cn_ref[0]

    def one_hot(first_row, nominal, width):
        rid = lax.broadcasted_iota(I32, (1, width), 1) + first_row
        ridf = rid.astype(F32)
        mine = (ridf >= start) & (ridf < end) & (rid >= nominal)
        return jnp.where(mine, 1.0, 0.0).astype(BF16)

    p = one_hot(rb, r0a, win)
    window_copy(idx, slot).wait()
    acc[...] = jnp.dot(p, zwin[slot].astype(BF16), preferred_element_type=F32)

    def chunk(c, carry):
        nominal = r0a + win + c * kc
        cb = pl.multiple_of(jnp.minimum(nominal, zrows - kc), SUB)
        cp = pltpu.make_async_copy(z_ref.at[pl.program_id(0), pl.ds(cb, kc), :], zbuf, sem)
        cp.start()
        pc = one_hot(cb, nominal, kc)
        cp.wait()
        acc[...] += jnp.dot(pc, zbuf[...].astype(BF16), preferred_element_type=F32)
        return carry

    lax.fori_loop(0, jnp.maximum(r1 - (r0a + win) + kc - 1, 0) // kc, chunk, 0)
    y = x1_ref[0] + g2_ref[0] * acc[...]
    o_ref[0] = y * lax.rsqrt(jnp.mean(y * y, axis=-1, keepdims=True) + EPS) * fg_ref[...]


def _combine(r0, r1, z, start_col, cnt_col, x1, g2, final_g, t, win, kc):
    b, n, d = x1.shape
    nt = n // t
    col = pl.BlockSpec((1, t, 1), lambda i, j, *_: (i, j, 0))
    grid_spec = pltpu.PrefetchScalarGridSpec(
        num_scalar_prefetch=2,
        grid=(b, nt),
        in_specs=[pl.BlockSpec(memory_space=pl.ANY), col, col,
                  pl.BlockSpec((1, t, d), lambda i, j, *_: (i, j, 0)),
                  pl.BlockSpec((1, 1, d), lambda i, j, *_: (i, 0, 0)),
                  pl.BlockSpec((1, d), lambda i, j, *_: (0, 0))],
        out_specs=pl.BlockSpec((1, t, d), lambda i, j, *_: (i, j, 0)),
        scratch_shapes=[pltpu.VMEM((2, win, d), F32), pltpu.VMEM((kc, d), F32), pltpu.VMEM((t, d), F32),
                        pltpu.SemaphoreType.DMA((2,)), pltpu.SemaphoreType.DMA],
    )
    return pl.pallas_call(
        _combine_kernel,
        out_shape=jax.ShapeDtypeStruct((b, n, d), F32),
        grid_spec=grid_spec,
        compiler_params=_params("arbitrary", "arbitrary"),
        name="combine",
    )(r0, r1, z, start_col, cnt_col, x1, g2, final_g.reshape(1, d))


def kernel(x, c, ctx, c_ctx, w_ada, b_ada, norm1_g, w_in, w_a2_f, b_a2_f, w_a2_b, b_a2_b,
           gla_norm_g, w_out, norm2_g, w_router, w_e_gate, w_e_up, w_e_down, final_norm_g):
    b, n, d = x.shape
    depth = w_ada.shape[0]
    assert depth == 1, "the context stream update of non-final layers is not implemented"
    assert n % (LANES * SUB) == 0 and d % LANES == 0
    hk, hv = GLA_HEADS * GLA_DK, GLA_HEADS * GLA_DV
    nk = n // LANES
    cap = CAPACITY_FACTOR * n // N_EXPERTS
    lyr = 0

    cond = jnp.zeros((SUB, d), F32).at[:b].set(c).at[b].set(c_ctx)
    ada = _ada(cond, w_ada[lyr], b_ada[lyr])
    sh1, sc1, g1, sh2, sc2, g2 = [ada[:b, j * d:(j + 1) * d].reshape(b, 1, d) for j in range(6)]
    sh1_c, sc1_c = [jnp.broadcast_to(ada[b, j * d:(j + 1) * d], (b, 1, d)) for j in range(2)]

    w = w_in[lyr]
    s_main = 2 * hk + 2 * hv
    w_main = w[:, :s_main].astype(BF16)
    w_z = jnp.pad(w[:, s_main:s_main + 2 * GATE_RANK], ((0, 0), (0, LANES - 2 * GATE_RANK))).astype(BF16)
    w_u = w[:, s_main + 2 * GATE_RANK:].astype(BF16)
    main_c, _, z_c = _inproj(ctx, norm1_g[lyr], sh1_c, sc1_c, w_main, w_u, w_z, tm=ctx.shape[1])
    main_l, u_l, z_l = _inproj(x, norm1_g[lyr], sh1, sc1, w_main, w_u, w_z, tm=512)

    zero = jnp.zeros((b, GLA_HEADS, GLA_DK, GLA_DV), F32)
    gates_f = _gate_weights(w_a2_f[lyr], b_a2_f[lyr], 0)
    gates_b = _gate_weights(w_a2_b[lyr], b_a2_b[lyr], GATE_RANK)
    _, _, s_f, s_b = _gla(main_c, z_c, gates_f, gates_b, zero, zero, r=ctx.shape[1])
    o_f, o_b, _, _ = _gla(main_l, z_l, gates_f, gates_b, s_f, s_b, r=256)

    w1, w2, csm = _dft_tables(n)
    yf4 = _four_b(_four_a(u_l, w1), w2, csm)

    x1, aff_t = _mix(o_f, o_b, main_l, yf4, x, gla_norm_g[lyr], w_out[lyr].astype(BF16), g1,
                      norm2_g[lyr], sh2, sc2, w_router[lyr].T, tm=512)

    a_tab = aff_t.reshape(b, N_EXPERTS, nk, LANES).transpose(0, 3, 1, 2).reshape(b, LANES, N_EXPERTS * nk)
    cum, pos, start, cnt = _route_a(a_tab, nk, cap)
    tok, gate, dst = _route_b(cum, a_tab, pos, nk, cap)

    z = _expert(tok, dst, x1, gate.reshape(b, N_EXPERTS, cap, 1), norm2_g[lyr], sh2, sc2,
                w_e_gate[lyr].astype(BF16), w_e_up[lyr].astype(BF16), w_e_down[lyr].astype(BF16), m=256)

    t_tile = 512
    start_tok = start.transpose(0, 2, 1).reshape(b, n)
    cnt_tok = cnt.transpose(0, 2, 1).reshape(b, n)
    r0 = start_tok[:, ::t_tile].astype(I32)
    r1 = jnp.concatenate([r0[:, 1:], jnp.full((b, 1), N_EXPERTS * cap, I32)], axis=1)
    return _combine(r0.reshape(-1), r1.reshape(-1), z, start_tok.reshape(b, n, 1), cnt_tok.reshape(b, n, 1),
                    x1, g2, final_norm_g, t=t_tile, win=5 * t_tile // 2, kc=256)
```
